```python
import jax, jax.numpy as jnp
from jax import lax
import numpy as np

D_MODEL = 4096
BATCH = 1
SEQ = 8192
DEPTH = 1

RMS_EPS = 1e-6

RWKV_HEADS = 32
RWKV_HEAD_DIM = 64
RWKV_WIDTH = RWKV_HEADS * RWKV_HEAD_DIM
DECAY_LORA = 64
ICLR_LORA = 64
GATE_LORA = 256
GN_EPS = 64e-5
RWKV_COLS = 3 * RWKV_WIDTH + DECAY_LORA + ICLR_LORA + GATE_LORA
RWKV_SPLITS = (RWKV_WIDTH, 2 * RWKV_WIDTH, 3 * RWKV_WIDTH,
               3 * RWKV_WIDTH + DECAY_LORA, 3 * RWKV_WIDTH + DECAY_LORA + ICLR_LORA)

MOBA_HEADS = 16
MOBA_HEAD_DIM = 128
MOBA_WIDTH = MOBA_HEADS * MOBA_HEAD_DIM
MOBA_BLOCK = 256
MOBA_TOPK = 3
MOBA_QCHUNK = 64
MOBA_COLS = 3 * MOBA_WIDTH

GATE_COLS = 2 * D_MODEL
IN_COLS = RWKV_COLS + MOBA_COLS + GATE_COLS

PEER_HEADS = 8
PEER_NKEYS = 128
PEER_EXPERTS = PEER_NKEYS * PEER_NKEYS
PEER_QDIM = 256
PEER_HALF = PEER_QDIM // 2
PEER_TOPK = 16
PEER_TCHUNK = 64

kernel_name = "hybrid_rwkv7_moba_peer_gated"


def rms_norm(x, g):
    xf = x.astype(jnp.float32)
    y = xf * lax.rsqrt(jnp.mean(xf * xf, axis=-1, keepdims=True) + RMS_EPS)
    return (y * g.astype(jnp.float32)).astype(x.dtype)


def _rwkv7_step(state, inp):
    r_t, w_t, k_t, v_t, a_t, b_t = inp
    sa = jnp.einsum('bhvk,bhk->bhv', state, a_t)
    state = (state * w_t[:, :, None, :] + sa[..., None] * b_t[:, :, None, :]
             + v_t[..., None] * k_t[:, :, None, :])
    y = jnp.einsum('bhvk,bhk->bhv', state, r_t)
    return state, y


def rwkv7_time_mix(z, mu, w0, w2, a0, a2, g2, k_k, k_a, r_k, gn_w, gn_b):
    B, S, _ = z.shape
    H, N = RWKV_HEADS, RWKV_HEAD_DIM
    out_dtype = z.dtype
    z = z.astype(jnp.float32)
    z_prev = jnp.pad(z, ((0, 0), (1, 0), (0, 0)))[:, :S]
    z = z + (z_prev - z) * mu
    r, k, v, wd, ad, gd = jnp.split(z, RWKV_SPLITS, axis=-1)
    w_log = -jax.nn.softplus(-(w0 + jnp.tanh(wd) @ w2)) - 0.5
    decay = jnp.exp(-jnp.exp(w_log))
    iclr = jax.nn.sigmoid(a0 + ad @ a2)
    g = jax.nn.sigmoid(gd) @ g2
    kk = (k * k_k).reshape(B, S, H, N)
    kk = kk * lax.rsqrt(jnp.sum(kk * kk, axis=-1, keepdims=True) + 1e-12)
    k = k * (1.0 + (iclr - 1.0) * k_a)
    r_h, k_h, v_h, w_h, a_h = (t.reshape(B, S, H, N) for t in (r, k, v, decay, iclr))
    xs = tuple(t.transpose(1, 0, 2, 3) for t in (r_h, w_h, k_h, v_h, -kk, kk * a_h))
    state0 = jnp.zeros((B, H, N, N), jnp.float32)
    _, y = lax.scan(_rwkv7_step, state0, xs)
    y = y.transpose(1, 0, 2, 3)
    mean = jnp.mean(y, axis=-1, keepdims=True)
    var = jnp.mean(jnp.square(y - mean), axis=-1, keepdims=True)
    y = ((y - mean) * lax.rsqrt(var + GN_EPS)).reshape(B, S, H * N) * gn_w + gn_b
    bonus = jnp.sum(r_h * k_h * r_k, axis=-1, keepdims=True) * v_h
    y = (y + bonus.reshape(B, S, H * N)) * g
    return y.astype(out_dtype)


def moba_attention(q, k, v):
    B, S, _ = q.shape
    H, D, BS, QC = MOBA_HEADS, MOBA_HEAD_DIM, MOBA_BLOCK, MOBA_QCHUNK
    NB = -(-S // BS)
    S_pad = NB * BS

    def heads(t):
        t = jnp.pad(t, ((0, 0), (0, S_pad - S), (0, 0)))
        return t.reshape(B, S_pad, H, D).transpose(0, 2, 1, 3)

    q, k, v = heads(q), heads(k), heads(v)
    kb = k.reshape(B, H, NB, BS, D)
    vb = v.reshape(B, H, NB, BS, D)
    k_mean = jnp.mean(kb.astype(jnp.float32), axis=3)
    gate = jnp.einsum('bhsd,bhnd->bhsn', q.astype(jnp.float32), k_mean)
    q_blk = jnp.arange(S_pad) // BS
    fully_past = jnp.arange(NB)[None, :] < q_blk[:, None]
    gate = jnp.where(fully_past, gate, -jnp.inf)
    n_sel = min(MOBA_TOPK, NB)
    _, sel = lax.top_k(gate, n_sel)
    valid = jnp.arange(n_sel)[None, :] < q_blk[:, None]
    n_chunks = S_pad // QC
    q_c = q.reshape(B, H, n_chunks, QC, D).transpose(2, 0, 1, 3, 4)
    sel_c = sel.reshape(B, H, n_chunks, QC, n_sel).transpose(2, 0, 1, 3, 4)
    valid_c = valid.reshape(n_chunks, QC, n_sel)
    bi = jnp.arange(B)[:, None, None, None]
    hi = jnp.arange(H)[None, :, None, None]
    scale = D ** -0.5

    def one_chunk(args):
        c, qc, selc, validc = args
        pos = c * QC + jnp.arange(QC)
        own = (c * QC) // BS
        k_sel = kb[bi, hi, selc]
        v_sel = vb[bi, hi, selc]
        s_past = jnp.einsum('bhqd,bhqjnd->bhqjn', qc, k_sel) * scale
        s_past = jnp.where(validc[None, None, :, :, None], s_past, -jnp.inf)
        k_own = lax.dynamic_index_in_dim(kb, own, axis=2, keepdims=False)
        v_own = lax.dynamic_index_in_dim(vb, own, axis=2, keepdims=False)
        s_own = jnp.einsum('bhqd,bhnd->bhqn', qc, k_own) * scale
        kpos = own * BS + jnp.arange(BS)
        s_own = jnp.where(kpos[None, :] <= pos[:, None], s_own, -jnp.inf)
        s = jnp.concatenate([s_past.reshape(B, H, QC, n_sel * BS), s_own], axis=-1)
        p = jax.nn.softmax(s.astype(jnp.float32), axis=-1).astype(v.dtype)
        p_past = p[..., :n_sel * BS].reshape(B, H, QC, n_sel, BS)
        p_own = p[..., n_sel * BS:]
        return (jnp.einsum('bhqjn,bhqjnd->bhqd', p_past, v_sel)
                + jnp.einsum('bhqn,bhnd->bhqd', p_own, v_own))

    o = lax.map(one_chunk, (jnp.arange(n_chunks), q_c, sel_c, valid_c))
    o = o.transpose(1, 2, 0, 3, 4).reshape(B, H, S_pad, D).transpose(0, 2, 1, 3)
    return o.reshape(B, S_pad, H * D)[:, :S]


def peer_ffn(x, w_q, sub_keys, u_tab, v_tab):
    B, S, Dm = x.shape
    qry = (x @ w_q).reshape(B, S, PEER_HEADS, 2, PEER_HALF)
    scores = jnp.einsum('bshcd,hcnd->bshcn', qry, sub_keys).astype(jnp.float32)
    s_top, i_top = lax.top_k(scores, PEER_TOPK)
    cand = (s_top[..., 0, :, None] + s_top[..., 1, None, :]).reshape(B, S, PEER_HEADS, PEER_TOPK ** 2)
    cand_idx = (i_top[..., 0, :, None] * PEER_NKEYS + i_top[..., 1, None, :]).reshape(B, S, PEER_HEADS, PEER_TOPK ** 2)
    best, pos = lax.top_k(cand, PEER_TOPK)
    expert = jnp.take_along_axis(cand_idx, pos, axis=-1)
    gate = jax.nn.softmax(best, axis=-1)
    T = B * S
    E = PEER_HEADS * PEER_TOPK
    n_chunks = T // PEER_TCHUNK
    xf = x.reshape(n_chunks, PEER_TCHUNK, Dm)
    ef = expert.reshape(n_chunks, PEER_TCHUNK, E)
    gf = gate.reshape(n_chunks, PEER_TCHUNK, E)

    def one_chunk(args):
        xc, ec, gc = args
        u = u_tab[ec]
        hid = jax.nn.gelu(jnp.einsum('td,ted->te', xc, u).astype(jnp.float32), approximate=False)
        return jnp.einsum('te,ted->td', (gc * hid).astype(xc.dtype), v_tab[ec])

    y = lax.map(one_chunk, (xf, ef, gf))
    return y.reshape(B, S, Dm)


def setup_inputs(seed: int = 0) -> dict:
    key = jax.random.key(seed)
    ks = jax.random.split(key, 24)
    f32 = jnp.float32
    L = DEPTH

    def nrm(k, shape, scale):
        return jax.random.normal(k, shape, f32) * scale

    return {
        "x": nrm(ks[0], (BATCH, SEQ, D_MODEL), 1.0),
        "ln1_g": 1.0 + nrm(ks[1], (L, D_MODEL), 0.02),
        "w_in": nrm(ks[2], (L, D_MODEL, IN_COLS), D_MODEL ** -0.5),
        "rwkv_mu": jax.random.uniform(ks[3], (L, RWKV_COLS), f32),
        "rwkv_w0": jax.random.uniform(ks[4], (L, RWKV_WIDTH), f32, -6.0, -1.0),
        "rwkv_w2": nrm(ks[5], (L, DECAY_LORA, RWKV_WIDTH), 0.5 * DECAY_LORA ** -0.5),
        "rwkv_a0": nrm(ks[6], (L, RWKV_WIDTH), 0.1),
        "rwkv_a2": nrm(ks[7], (L, ICLR_LORA, RWKV_WIDTH), 0.5 * ICLR_LORA ** -0.5),
        "rwkv_g2": nrm(ks[8], (L, GATE_LORA, RWKV_WIDTH), GATE_LORA ** -0.5),
        "rwkv_k_k": 0.85 + nrm(ks[9], (L, RWKV_WIDTH), 0.02),
        "rwkv_k_a": 1.0 + nrm(ks[10], (L, RWKV_WIDTH), 0.02),
        "rwkv_r_k": nrm(ks[11], (L, RWKV_HEADS, RWKV_HEAD_DIM), 0.1),
        "rwkv_gn_w": 1.0 + nrm(ks[12], (L, RWKV_WIDTH), 0.02),
        "rwkv_gn_b": nrm(ks[13], (L, RWKV_WIDTH), 0.02),
        "w_proj_rwkv": nrm(ks[14], (L, RWKV_WIDTH, D_MODEL), RWKV_WIDTH ** -0.5),
        "w_proj_moba": nrm(ks[15], (L, MOBA_WIDTH, D_MODEL), MOBA_WIDTH ** -0.5),
        "w_out": nrm(ks[16], (L, D_MODEL, D_MODEL), D_MODEL ** -0.5),
        "ln2_g": 1.0 + nrm(ks[17], (L, D_MODEL), 0.02),
        "peer_w_q": nrm(ks[18], (L, D_MODEL, PEER_HEADS * PEER_QDIM), D_MODEL ** -0.5),
        "peer_sub_keys": nrm(ks[19], (L, PEER_HEADS, 2, PEER_NKEYS, PEER_HALF), PEER_HALF ** -0.5),
        "peer_u": nrm(ks[20], (L, PEER_EXPERTS, D_MODEL), D_MODEL ** -0.5),
        "peer_v": nrm(ks[21], (L, PEER_EXPERTS, D_MODEL), PEER_HEADS ** -0.5),
        "ln_f_g": 1.0 + nrm(ks[22], (D_MODEL,), 0.02),
    }


def reference(x, ln1_g, w_in, rwkv_mu, rwkv_w0, rwkv_w2, rwkv_a0, rwkv_a2, rwkv_g2,
              rwkv_k_k, rwkv_k_a, rwkv_r_k, rwkv_gn_w, rwkv_gn_b,
              w_proj_rwkv, w_proj_moba, w_out, ln2_g,
              peer_w_q, peer_sub_keys, peer_u, peer_v, ln_f_g):
    h = x
    for l in range(DEPTH):
        xn = rms_norm(h, ln1_g[l])
        z = xn @ w_in[l]
        z_rwkv = z[..., :RWKV_COLS]
        z_moba = z[..., RWKV_COLS:RWKV_COLS + MOBA_COLS]
        z_gate = z[..., RWKV_COLS + MOBA_COLS:]
        y_a = rwkv7_time_mix(z_rwkv, rwkv_mu[l], rwkv_w0[l], rwkv_w2[l], rwkv_a0[l], rwkv_a2[l],
                             rwkv_g2[l], rwkv_k_k[l], rwkv_k_a[l], rwkv_r_k[l],
                             rwkv_gn_w[l], rwkv_gn_b[l])
        q_b, k_b, v_b = jnp.split(z_moba, 3, axis=-1)
        y_b = moba_attention(q_b, k_b, v_b)
        gate_a, gate_b = jnp.split(jax.nn.sigmoid(z_gate), 2, axis=-1)
        merged = gate_a * (y_a @ w_proj_rwkv[l]) + gate_b * (y_b @ w_proj_moba[l])
        h = h + merged @ w_out[l]
        hn = rms_norm(h, ln2_g[l])
        h = h + peer_ffn(hn, peer_w_q[l], peer_sub_keys[l], peer_u[l], peer_v[l])
    return rms_norm(h, ln_f_g)
```

```python
import functools

import jax
import jax.numpy as jnp
from jax import lax
from jax.experimental import pallas as pl
from jax.experimental.pallas import tpu as pltpu

F32 = jnp.float32
BF16 = jnp.bfloat16

D_MODEL = 4096
RMS_EPS = 1e-6

RWKV_HEADS = 32
RWKV_HEAD_DIM = 64
RWKV_WIDTH = RWKV_HEADS * RWKV_HEAD_DIM
DECAY_LORA = 64
ICLR_LORA = 64
GATE_LORA = 256
LORA_COLS = DECAY_LORA + ICLR_LORA + GATE_LORA
GN_EPS = 64e-5
RWKV_COLS = 3 * RWKV_WIDTH + LORA_COLS
CHUNK = 64
PAIR = 2 * RWKV_HEAD_DIM
N_PAIRS = RWKV_HEADS // 2

MOBA_HEADS = 16
MOBA_HEAD_DIM = 128
MOBA_WIDTH = MOBA_HEADS * MOBA_HEAD_DIM
MOBA_BLOCK = 256
MOBA_TOPK = 3
MOBA_COLS = 3 * MOBA_WIDTH

PEER_HEADS = 8
PEER_NKEYS = 128
PEER_EXPERTS = PEER_NKEYS * PEER_NKEYS
PEER_QDIM = 256
PEER_HALF = PEER_QDIM // 2
PEER_TOPK = 16

VMEM_LIMIT = 56 * 1024 * 1024

NEG_INF = float("-inf")


def _params(sem):
    return pltpu.CompilerParams(dimension_semantics=sem, vmem_limit_bytes=VMEM_LIMIT)


def _dot(a, b):
    return jnp.dot(a, b, preferred_element_type=F32)


def _dot_nt(a, b):
    return lax.dot_general(a, b, (((1,), (1,)), ((), ())), preferred_element_type=F32)


def _dot_tn(a, b):
    return lax.dot_general(a, b, (((0,), (0,)), ((), ())), preferred_element_type=F32)


def _split3(x):
    hi = x.astype(BF16)
    r1 = x - hi.astype(F32)
    mid = r1.astype(BF16)
    lo = (r1 - mid.astype(F32)).astype(BF16)
    return hi, mid, lo


def _dot_exact_rhs(x, e):
    hi, mid, lo = _split3(x)
    return _dot(hi, e) + _dot(mid, e) + _dot(lo, e)


def _dot_exact_lhs(e, x):
    hi, mid, lo = _split3(x)
    return _dot(e, hi) + _dot(e, mid) + _dot(e, lo)


def _rmsnorm_kernel(x_ref, g_ref, o_ref):
    x = x_ref[...]
    y = x * lax.rsqrt(jnp.mean(x * x, axis=-1, keepdims=True) + RMS_EPS)
    o_ref[...] = (y * g_ref[...]).astype(o_ref.dtype)


def rmsnorm(x, g, out_dtype, bm=256):
    t, d = x.shape
    return pl.pallas_call(
        _rmsnorm_kernel,
        grid=(t // bm,),
        in_specs=[pl.BlockSpec((bm, d), lambda i: (i, 0)), pl.BlockSpec((1, d), lambda i: (0, 0))],
        out_specs=pl.BlockSpec((bm, d), lambda i: (i, 0)),
        out_shape=jax.ShapeDtypeStruct((t, d), out_dtype),
        compiler_params=_params(("parallel",)),
        name="rmsnorm",
    )(x, g.reshape(1, d))


def _add_rmsnorm_kernel(a_ref, b_ref, g_ref, o_ref):
    x = a_ref[...] + b_ref[...]
    y = x * lax.rsqrt(jnp.mean(x * x, axis=-1, keepdims=True) + RMS_EPS)
    o_ref[...] = (y * g_ref[...]).astype(o_ref.dtype)


def add_rmsnorm(a, b, g, bm=256):
    t, d = a.shape
    return pl.pallas_call(
        _add_rmsnorm_kernel,
        grid=(t // bm,),
        in_specs=[pl.BlockSpec((bm, d), lambda i: (i, 0)), pl.BlockSpec((bm, d), lambda i: (i, 0)),
                  pl.BlockSpec((1, d), lambda i: (0, 0))],
        out_specs=pl.BlockSpec((bm, d), lambda i: (i, 0)),
        out_shape=jax.ShapeDtypeStruct((t, d), F32),
        compiler_params=_params(("parallel",)),
        name="add_rmsnorm",
    )(a, b, g.reshape(1, d))


def _mm_kernel(a_ref, b_ref, o_ref):
    o_ref[...] = _dot(a_ref[...], b_ref[...]).astype(o_ref.dtype)


def matmul(a, b, out_dtype, bm, bn, name):
    m, k = a.shape
    n = b.shape[1]
    bm, bn = min(bm, m), min(bn, n)
    return pl.pallas_call(
        _mm_kernel,
        grid=(n // bn, m // bm),
        in_specs=[pl.BlockSpec((bm, k), lambda j, i: (i, 0)), pl.BlockSpec((k, bn), lambda j, i: (0, j))],
        out_specs=pl.BlockSpec((bm, bn), lambda j, i: (i, j)),
        out_shape=jax.ShapeDtypeStruct((m, n), out_dtype),
        compiler_params=_params(("parallel", "parallel")),
        name=name,
    )(a, b)


def _merge_kernel(ya_ref, yb_ref, wa_ref, wb_ref, ga_ref, gb_ref, o_ref):
    pa = _dot(ya_ref[...], wa_ref[...])
    pb = _dot(yb_ref[...], wb_ref[...])
    o_ref[...] = (jax.nn.sigmoid(ga_ref[...]) * pa + jax.nn.sigmoid(gb_ref[...]) * pb).astype(o_ref.dtype)


def gated_merge(y_a, y_b, w_a, w_b, z_gate, bm=512, bn=1024):
    m, k = y_a.shape
    n = w_a.shape[1]
    bm = min(bm, m)
    nb = n // bn
    return pl.pallas_call(
        _merge_kernel,
        grid=(nb, m // bm),
        in_specs=[pl.BlockSpec((bm, k), lambda j, i: (i, 0)), pl.BlockSpec((bm, k), lambda j, i: (i, 0)),
                  pl.BlockSpec((k, bn), lambda j, i: (0, j)), pl.BlockSpec((k, bn), lambda j, i: (0, j)),
                  pl.BlockSpec((bm, bn), lambda j, i: (i, j)), pl.BlockSpec((bm, bn), lambda j, i: (i, j + nb))],
        out_specs=pl.BlockSpec((bm, bn), lambda j, i: (i, j)),
        out_shape=jax.ShapeDtypeStruct((m, n), BF16),
        compiler_params=_params(("parallel", "parallel")),
        name="gated_merge",
    )(y_a, y_b, w_a, w_b, z_gate, z_gate)


def _mm_residual_kernel(a_ref, b_ref, r_ref, o_ref):
    o_ref[...] = r_ref[...] + _dot(a_ref[...], b_ref[...])


def matmul_residual(a, b, res, bm=1024, bn=1024):
    m, k = a.shape
    n = b.shape[1]
    bm = min(bm, m)
    return pl.pallas_call(
        _mm_residual_kernel,
        grid=(n // bn, m // bm),
        in_specs=[pl.BlockSpec((bm, k), lambda j, i: (i, 0)), pl.BlockSpec((k, bn), lambda j, i: (0, j)),
                  pl.BlockSpec((bm, bn), lambda j, i: (i, j))],
        out_specs=pl.BlockSpec((bm, bn), lambda j, i: (i, j)),
        out_shape=jax.ShapeDtypeStruct((m, n), F32),
        compiler_params=_params(("parallel", "parallel")),
        name="out_proj_residual",
    )(a, b, res)


def _softplus(x):
    return jnp.maximum(x, 0.0) + jnp.log1p(jnp.exp(-jnp.abs(x)))


def _pair_masks(shape):
    lane = lax.broadcasted_iota(jnp.int32, shape, len(shape) - 1)
    return lane < RWKV_HEAD_DIM


def _block_ones():
    r = lax.broadcasted_iota(jnp.int32, (PAIR, PAIR), 0) // RWKV_HEAD_DIM
    c = lax.broadcasted_iota(jnp.int32, (PAIR, PAIR), 1) // RWKV_HEAD_DIM
    return jnp.where(r == c, 1.0, 0.0).astype(BF16)


def _rwkv_local_kernel(zr_ref, zk_ref, zv_ref, zl_ref, pr_ref, pk_ref, pv_ref, pl_ref,
                       mur_ref, muk_ref, muv_ref, mul_ref,
                       w0_ref, w2_ref, a0_ref, a2_ref, g2_ref, kk_ref, ka_ref, rk_ref,
                       ar_ref, vt_ref, y0_ref, arb_ref, bh_ref, kvt_ref, gam_ref, g_ref, bonus_ref):
    first = pl.program_id(1) == 0
    tm = zr_ref.shape[0]
    nc = tm // CHUNK

    def shifted(z_ref, p_ref, mu_ref):
        z = z_ref[...]
        prev_last = jnp.where(first, 0.0, p_ref[7:8, :])
        row = lax.broadcasted_iota(jnp.int32, z.shape, 0)
        zp = jnp.where(row == 0, prev_last, pltpu.roll(z, 1, axis=0))
        return z + (zp - z) * mu_ref[...]

    r = shifted(zr_ref, pr_ref, mur_ref)
    k = shifted(zk_ref, pk_ref, muk_ref)
    v = shifted(zv_ref, pv_ref, muv_ref)
    zl = shifted(zl_ref, pl_ref, mul_ref)
    wd = zl[:, :DECAY_LORA]
    ad = zl[:, DECAY_LORA:DECAY_LORA + ICLR_LORA]
    gd = zl[:, DECAY_LORA + ICLR_LORA:]

    wl = w0_ref[...] + _dot(jnp.tanh(wd).astype(BF16), w2_ref[...])
    logw = -jnp.exp(-_softplus(-wl) - 0.5)
    iclr = jax.nn.sigmoid(a0_ref[...] + _dot(ad.astype(BF16), a2_ref[...]))
    g_ref[...] = _dot(jax.nn.sigmoid(gd).astype(BF16), g2_ref[...])

    ones_blk = _block_ones()
    kk = k * kk_ref[...]
    kk = kk * lax.rsqrt(_dot_exact_rhs(kk * kk, ones_blk) + 1e-12)
    k = k * (1.0 + (iclr - 1.0) * ka_ref[...])
    beta = kk * iclr
    bonus_ref[...] = _dot_exact_rhs(r * k * rk_ref[...], ones_blk) * v

    ti = lax.broadcasted_iota(jnp.int32, (CHUNK, CHUNK), 0)
    si = lax.broadcasted_iota(jnp.int32, (CHUNK, CHUNK), 1)
    strict = ti > si
    incl = ti >= si
    tri_ones = jnp.where(incl, 1.0, 0.0).astype(BF16)
    eye = jnp.where(ti == si, 1.0, 0.0)
    head0 = _pair_masks((CHUNK, PAIR))
    head0_x = _pair_masks((2 * CHUNK, PAIR))
    vr = lax.broadcasted_iota(jnp.int32, (PAIR, PAIR), 0) // RWKV_HEAD_DIM
    vc = lax.broadcasted_iota(jnp.int32, (PAIR, PAIR), 1) // RWKV_HEAD_DIM
    same_head = vr == vc

    for c in range(nc):
        rows = slice(c * CHUNK, (c + 1) * CHUNK)
        lw = logw[rows]
        cs = _dot_exact_lhs(tri_ones, lw)
        cs_last = cs[CHUNK - 1:CHUNK, :]
        e_neg = jnp.exp(-cs)
        e_hat = jnp.exp(cs_last - cs)
        abar = -kk[rows] * jnp.exp(cs - lw)
        rbar = r[rows] * jnp.exp(cs)
        bbar = (beta[rows] * e_neg).astype(BF16)
        kbar = (k[rows] * e_neg).astype(BF16)
        vc_ = v[rows]
        vb = vc_.astype(BF16)
        x = jnp.concatenate([abar, rbar], axis=0)
        abar_b = abar.astype(BF16)

        at_h, vt_h, y0_h, arb_h = [], [], [], []
        for h in range(2):
            hm = head0_x if h == 0 else jnp.logical_not(head0_x)
            xh = jnp.where(hm, x, 0.0).astype(BF16)
            mb = _dot_nt(xh, bbar)
            mk = _dot_nt(xh, kbar)
            n = jnp.where(strict, mb[:CHUNK], 0.0)
            a_ak = jnp.where(strict, mk[:CHUNK], 0.0)
            a_rb = jnp.where(incl, mb[CHUNK:], 0.0)
            a_rk = jnp.where(incl, mk[CHUNK:], 0.0)
            tinv = eye + n
            npow = n
            for _ in range(5):
                nb = npow.astype(BF16)
                npow = _dot(nb, nb)
                tinv = tinv + _dot(npow.astype(BF16), tinv.astype(BF16))
            tb = tinv.astype(BF16)
            at_h.append(_dot(tb, abar_b))
            w_h = _dot(a_ak.astype(BF16), vb)
            vt_h.append(_dot(tb, w_h.astype(BF16)))
            y0_h.append(_dot(a_rk.astype(BF16), vb))
            arb_h.append(a_rb)

        atil = jnp.where(head0, at_h[0], at_h[1])
        ar_ref[c] = jnp.concatenate([atil, rbar], axis=0).astype(BF16)
        vt_ref[rows, :] = jnp.where(head0, vt_h[0], vt_h[1])
        y0_ref[rows, :] = jnp.where(head0, y0_h[0], y0_h[1])
        arb_ref[rows, :] = jnp.concatenate(arb_h, axis=1).astype(BF16)
        bh = (beta[rows] * e_hat).astype(BF16)
        bh_ref[rows, :] = bh
        khat = (k[rows] * e_hat).astype(BF16)
        kvt_ref[c] = jnp.where(same_head, _dot_tn(vb, khat), 0.0)
        gam_ref[c] = jnp.exp(cs_last)


def rwkv_local(z_rkv, z_l, mu, w0, w2, a0, a2, g2, k_k, k_a, r_k, tm=256):
    t = z_rkv.shape[0]
    nc = tm // CHUNK
    n_chunks = t // CHUNK
    w = RWKV_WIDTH
    mu_rkv = mu[:3 * w].reshape(1, 3 * w)
    mu_l = mu[3 * w:].reshape(1, LORA_COLS)
    rb = tm // 8

    def zspec(off):
        return pl.BlockSpec((tm, PAIR), lambda p, i: (i, p + off))

    def pspec(off):
        return pl.BlockSpec((8, PAIR), lambda p, i: (jnp.maximum(i * rb - 1, 0), p + off))

    def vspec(off=0):
        return pl.BlockSpec((1, PAIR), lambda p, i: (0, p + off))

    row = lambda a: a.reshape(1, w)
    tok = pl.BlockSpec((tm, PAIR), lambda p, i: (i, p))
    chk = pl.BlockSpec((nc, PAIR, PAIR), lambda p, i: (i, 0, p))
    f32 = lambda s: jax.ShapeDtypeStruct(s, F32)
    b16 = lambda s: jax.ShapeDtypeStruct(s, BF16)
    return pl.pallas_call(
        _rwkv_local_kernel,
        grid=(N_PAIRS, t // tm),
        in_specs=[zspec(0), zspec(N_PAIRS), zspec(2 * N_PAIRS),
                  pl.BlockSpec((tm, LORA_COLS), lambda p, i: (i, 0)),
                  pspec(0), pspec(N_PAIRS), pspec(2 * N_PAIRS),
                  pl.BlockSpec((8, LORA_COLS), lambda p, i: (jnp.maximum(i * rb - 1, 0), 0)),
                  vspec(0), vspec(N_PAIRS), vspec(2 * N_PAIRS),
                  pl.BlockSpec((1, LORA_COLS), lambda p, i: (0, 0)),
                  vspec(), pl.BlockSpec((DECAY_LORA, PAIR), lambda p, i: (0, p)),
                  vspec(), pl.BlockSpec((ICLR_LORA, PAIR), lambda p, i: (0, p)),
                  pl.BlockSpec((GATE_LORA, PAIR), lambda p, i: (0, p)),
                  vspec(), vspec(), vspec()],
        out_specs=[chk, tok, tok, tok, tok, chk,
                   pl.BlockSpec((nc, 1, PAIR), lambda p, i: (i, 0, p)), tok, tok],
        out_shape=[b16((n_chunks, PAIR, w)), f32((t, w)), f32((t, w)), b16((t, w)), b16((t, w)),
                   f32((n_chunks, PAIR, w)), f32((n_chunks, 1, w)), f32((t, w)), f32((t, w))],
        compiler_params=_params(("parallel", "parallel")),
        name="rwkv_local",
    )(z_rkv, z_rkv, z_rkv, z_l, z_rkv, z_rkv, z_rkv, z_l,
      mu_rkv, mu_rkv, mu_rkv, mu_l,
      row(w0), w2.astype(BF16), row(a0), a2.astype(BF16), g2.astype(BF16), row(k_k), row(k_a), row(r_k))


def _rwkv_scan_kernel(ar_ref, vt_ref, y0_ref, arb_ref, bh_ref, kvt_ref, gam_ref, g_ref, bonus_ref,
                      gnw_ref, gnb_ref, y_ref, s_ref):
    @pl.when(pl.program_id(0) == 0)
    def _():
        s_ref[...] = jnp.zeros_like(s_ref)

    head0 = _pair_masks((CHUNK, PAIR))
    vr = lax.broadcasted_iota(jnp.int32, (PAIR, PAIR), 0) // RWKV_HEAD_DIM
    vc = lax.broadcasted_iota(jnp.int32, (PAIR, PAIR), 1) // RWKV_HEAD_DIM
    same_head = vr == vc
    ones_blk = _block_ones()
    inv_n = 1.0 / RWKV_HEAD_DIM

    for p in range(N_PAIRS):
        lanes = slice(p * PAIR, (p + 1) * PAIR)
        s = s_ref[p]
        xh = _dot_nt(ar_ref[0, :, lanes], s.astype(BF16))
        u = xh[:CHUNK] + vt_ref[:, lanes]
        ub = u.astype(BF16)
        arb = arb_ref[:, lanes]
        yu = jnp.where(head0, _dot(arb[:, :CHUNK], ub), _dot(arb[:, CHUNK:], ub))
        y = xh[CHUNK:] + yu + y0_ref[:, lanes]
        s_ref[p] = (s * gam_ref[0, :, lanes]
                    + jnp.where(same_head, _dot_tn(ub, bh_ref[:, lanes]), 0.0)
                    + kvt_ref[0, :, lanes])
        mean = _dot_exact_rhs(y, ones_blk) * inv_n
        yc = y - mean
        var = _dot_exact_rhs(yc * yc, ones_blk) * inv_n
        yn = yc * lax.rsqrt(var + GN_EPS) * gnw_ref[:, lanes] + gnb_ref[:, lanes]
        y_ref[:, lanes] = ((yn + bonus_ref[:, lanes]) * g_ref[:, lanes]).astype(y_ref.dtype)


def rwkv_scan(ar, vt, y0, arb, bh, kvt, gam, g, bonus, gn_w, gn_b):
    t, w = vt.shape
    n_chunks = t // CHUNK
    tok = pl.BlockSpec((CHUNK, w), lambda c: (c, 0))
    chk = pl.BlockSpec((1, PAIR, w), lambda c: (c, 0, 0))
    vec = pl.BlockSpec((1, w), lambda c: (0, 0))
    return pl.pallas_call(
        _rwkv_scan_kernel,
        grid=(n_chunks,),
        in_specs=[chk, tok, tok, tok, tok, chk, pl.BlockSpec((1, 1, w), lambda c: (c, 0, 0)), tok, tok, vec, vec],
        out_specs=tok,
        out_shape=jax.ShapeDtypeStruct((t, w), BF16),
        scratch_shapes=[pltpu.VMEM((N_PAIRS, PAIR, PAIR), F32)],
        compiler_params=_params(("arbitrary",)),
        name="rwkv_scan",
    )(ar, vt, y0, arb, bh, kvt, gam, g, bonus, gn_w.reshape(1, w), gn_b.reshape(1, w))


def rwkv7(z_rkv, z_l, mu, w0, w2, a0, a2, g2, k_k, k_a, r_k, gn_w, gn_b):
    ar, vt, y0, arb, bh, kvt, gam, g, bonus = rwkv_local(
        z_rkv, z_l, mu, w0, w2, a0, a2, g2, k_k, k_a, r_k.reshape(-1))
    return rwkv_scan(ar, vt, y0, arb, bh, kvt, gam, g, bonus, gn_w, gn_b)


def _block_mean_kernel(k_ref, o_ref):
    o_ref[0] = jnp.mean(k_ref[...].astype(F32), axis=0, keepdims=True)


def moba_block_means(z_moba):
    t = z_moba.shape[0]
    nb = t // MOBA_BLOCK
    return pl.pallas_call(
        _block_mean_kernel,
        grid=(nb,),
        in_specs=[pl.BlockSpec((MOBA_BLOCK, MOBA_WIDTH), lambda i: (i, 1))],
        out_specs=pl.BlockSpec((1, 1, MOBA_WIDTH), lambda i: (i, 0, 0)),
        out_shape=jax.ShapeDtypeStruct((nb, 1, MOBA_WIDTH), F32),
        compiler_params=_params(("parallel",)),
        name="moba_block_means",
    )(z_moba)


def _moba_kernel(q_ref, k_ref, v_ref, km_ref, o_ref):
    i = pl.program_id(1)
    bs = MOBA_BLOCK
    nb = km_ref.shape[0]
    q = q_ref[...]
    scale = MOBA_HEAD_DIM ** -0.5

    km_hi, km_mid, km_lo = _split3(km_ref[...])
    gate = _dot_nt(q, km_hi) + _dot_nt(q, km_mid) + _dot_nt(q, km_lo)
    col = lax.broadcasted_iota(jnp.int32, (bs, nb), 1)
    gate = jnp.where(col < i, gate, NEG_INF)
    sel = jnp.zeros((bs, nb), F32)
    for _ in range(MOBA_TOPK):
        m = jnp.max(gate, axis=1, keepdims=True)
        cand = jnp.where((gate == m) & (m > NEG_INF), col, nb)
        idx = jnp.min(cand, axis=1, keepdims=True)
        pick = col == idx
        sel = jnp.where(pick, 1.0, sel)
        gate = jnp.where(pick, NEG_INF, gate)

    own = pl.multiple_of(i * bs, bs)
    s = _dot_nt(q, k_ref[pl.ds(own, bs), :]) * scale
    qi = lax.broadcasted_iota(jnp.int32, (bs, bs), 0)
    ki = lax.broadcasted_iota(jnp.int32, (bs, bs), 1)
    s = jnp.where(ki <= qi, s, NEG_INF)
    m0 = jnp.max(s, axis=1, keepdims=True)
    p = jnp.exp(s - m0)
    l0 = jnp.sum(p, axis=1, keepdims=True)
    acc0 = _dot(p.astype(BF16), v_ref[pl.ds(own, bs), :])

    def body(n, carry):
        m, l, acc = carry
        start = pl.multiple_of(n * bs, bs)
        s = _dot_nt(q, k_ref[pl.ds(start, bs), :]) * scale
        picked = jnp.max(jnp.where(col == n, sel, 0.0), axis=1, keepdims=True)
        s = jnp.where(picked > 0.0, s, NEG_INF)
        m_new = jnp.maximum(m, jnp.max(s, axis=1, keepdims=True))
        alpha = jnp.exp(m - m_new)
        p = jnp.exp(s - m_new)
        l = alpha * l + jnp.sum(p, axis=1, keepdims=True)
        acc = alpha * acc + _dot(p.astype(BF16), v_ref[pl.ds(start, bs), :])
        return m_new, l, acc

    _, l, acc = lax.fori_loop(0, i, body, (m0, l0, acc0))
    o_ref[...] = (acc / l).astype(o_ref.dtype)


def _staircase_pairs():
    k = PEER_TOPK
    return [(a, b) for a in range(k) for b in range(k) if (a + 1) * (b + 1) <= k]


def _peer_route_kernel(q_ref, keys_ref, r2_ref, lb_ref, e1_ref, e2_ref, s_scr, rank_scr):
    tm = q_ref.shape[0]
    nk, k = PEER_NKEYS, PEER_TOPK
    rowi = lax.broadcasted_iota(jnp.int32, (nk, tm), 0)
    sub8 = lax.broadcasted_iota(jnp.int32, (PEER_HEADS, tm), 0)

    vals = [[jnp.zeros((PEER_HEADS, tm), F32) for _ in range(k)] for _ in range(2)]
    for hc in range(2 * PEER_HEADS):
        h, c = divmod(hc, 2)
        q_hi, q_mid, q_lo = _split3(q_ref[:, hc * PEER_HALF:(hc + 1) * PEER_HALF])
        kh = keys_ref[hc]
        s = _dot_nt(kh, q_hi) + _dot_nt(kh, q_mid) + _dot_nt(kh, q_lo)
        s_scr[hc] = s
        rank = jnp.full((nk, tm), float(k), F32)
        for j in range(k):
            m = jnp.max(s, axis=0, keepdims=True)
            idx = jnp.min(jnp.where(s == m, rowi, nk), axis=0, keepdims=True)
            pick = rowi == idx
            rank = jnp.where(pick, float(j), rank)
            s = jnp.where(pick, NEG_INF, s)
            vals[c][j] = jnp.where(sub8 == h, jnp.broadcast_to(m, (PEER_HEADS, tm)), vals[c][j])
        rank_scr[hc] = rank

    pairs = _staircase_pairs()
    csum = {p: vals[0][p[0]] + vals[1][p[1]] for p in pairs}
    cnt = {}
    for y in pairs:
        cnt[y] = jnp.full((PEER_HEADS, tm), float((y[0] + 1) * (y[1] + 1) - 1), F32)
    for xi, x in enumerate(pairs):
        for y in pairs[xi + 1:]:
            if x[0] <= y[0] and x[1] <= y[1]:
                continue
            x_first = jnp.where(csum[x] >= csum[y], 1.0, 0.0)
            cnt[y] = cnt[y] + x_first
            cnt[x] = cnt[x] + (1.0 - x_first)
    ex = [[jnp.exp(vals[c][j] - vals[c][0]) for j in range(k)] for c in range(2)]
    length = [jnp.zeros((PEER_HEADS, tm), F32) for _ in range(k)]
    z = jnp.zeros((PEER_HEADS, tm), F32)
    for p in pairs:
        chosen = jnp.where(cnt[p] < float(k), 1.0, 0.0)
        length[p[0]] = length[p[0]] + chosen
        z = z + chosen * (ex[0][p[0]] * ex[1][p[1]])
    inv_z = 1.0 / z

    for h in range(PEER_HEADS):
        r1 = rank_scr[2 * h]
        r2 = rank_scr[2 * h + 1]
        lb = jnp.zeros((nk, tm), F32)
        for a in range(k):
            lb = jnp.where(r1 == float(a), length[a][h:h + 1, :], lb)
        lb_ref[h] = lb
        e1_ref[h] = jnp.where(r1 < float(k), jnp.exp(s_scr[2 * h] - vals[0][0][h:h + 1, :]), 0.0)
        e2 = jnp.exp(s_scr[2 * h + 1] - vals[1][0][h:h + 1, :]) * inv_z[h:h + 1, :]
        e2_ref[h] = jnp.where(r2 < float(k), e2, 0.0).astype(e2_ref.dtype)
        r2_ref[h] = r2.astype(r2_ref.dtype)


def peer_route(qry, sub_keys, tm=256):
    t = qry.shape[0]
    tm = min(tm, t)
    nh, nk = PEER_HEADS, PEER_NKEYS
    keys = sub_keys.reshape(2 * nh, nk, PEER_HALF).astype(BF16)
    out = pl.BlockSpec((nh, nk, tm), lambda i: (0, 0, i))
    return pl.pallas_call(
        _peer_route_kernel,
        grid=(t // tm,),
        in_specs=[pl.BlockSpec((tm, 2 * nh * PEER_HALF), lambda i: (i, 0)),
                  pl.BlockSpec((2 * nh, nk, PEER_HALF), lambda i: (0, 0, 0))],
        out_specs=[out, out, out, out],
        out_shape=[jax.ShapeDtypeStruct((nh, nk, t), BF16), jax.ShapeDtypeStruct((nh, nk, t), F32),
                   jax.ShapeDtypeStruct((nh, nk, t), F32), jax.ShapeDtypeStruct((nh, nk, t), BF16)],
        scratch_shapes=[pltpu.VMEM((2 * nh, nk, tm), F32), pltpu.VMEM((2 * nh, nk, tm), F32)],
        compiler_params=_params(("parallel",)),
        name="peer_route",
    )(qry, keys)


PEER_I1_PER_TILE = 4


def _peer_expert_kernel(x_ref, u_ref, v_ref, r2_ref, lb_ref, e1_ref, e2_ref, o_ref):
    j = pl.program_id(1)
    tm = x_ref.shape[0]
    nk = PEER_NKEYS

    @pl.when(j == 0)
    def _():
        o_ref[...] = jnp.zeros_like(o_ref)

    hid = _dot_nt(u_ref[...], x_ref[...])
    act = 0.5 * hid * (1.0 + lax.erf(hid * (2.0 ** -0.5)))
    base = (j % (8 // PEER_I1_PER_TILE)) * PEER_I1_PER_TILE
    parts = []
    for a in range(PEER_I1_PER_TILE):
        g = jnp.zeros((nk, tm), BF16)
        for h in range(PEER_HEADS):
            lrow = lb_ref[h, pl.ds(base + a, 1), :]
            erow = e1_ref[h, pl.ds(base + a, 1), :]
            lbc = jnp.broadcast_to(lrow, (nk, tm)).astype(BF16)
            e1c = jnp.broadcast_to(erow, (nk, tm)).astype(BF16)
            g = g + jnp.where(r2_ref[h] < lbc, e1c * e2_ref[h], jnp.zeros((nk, tm), BF16))
        parts.append(g)
    gate = jnp.concatenate(parts, axis=0)
    p = (act * gate.astype(F32)).astype(BF16)
    o_ref[...] += _dot_tn(p, v_ref[...])


def peer_experts(hn, u_tab, v_tab, r2, lb, e1, e2, tm=512):
    t, d = hn.shape
    tm = min(tm, t)
    te = PEER_I1_PER_TILE * PEER_NKEYS
    nh, nk = PEER_HEADS, PEER_NKEYS
    full = pl.BlockSpec((nh, nk, tm), lambda i, j: (0, 0, i))
    rows = pl.BlockSpec((nh, 8, tm), lambda i, j: (0, j // (8 // PEER_I1_PER_TILE), i))
    return pl.pallas_call(
        _peer_expert_kernel,
        grid=(t // tm, PEER_EXPERTS // te),
        in_specs=[pl.BlockSpec((tm, d), lambda i, j: (i, 0)),
                  pl.BlockSpec((te, d), lambda i, j: (j, 0)),
                  pl.BlockSpec((te, d), lambda i, j: (j, 0)),
                  full, rows, rows, full],
        out_specs=pl.BlockSpec((tm, d), lambda i, j: (i, 0)),
        out_shape=jax.ShapeDtypeStruct((t, d), F32),
        compiler_params=_params(("parallel", "arbitrary")),
        name="peer_experts",
    )(hn, u_tab, v_tab, r2, lb, e1, e2)


def peer_ffn(hn, w_q, sub_keys, u_tab, v_tab):
    qry = matmul(hn, w_q, F32, 1024, 1024, "peer_query")
    r2, lb, e1, e2 = peer_route(qry, sub_keys)
    return peer_experts(hn, u_tab, v_tab, r2, lb, e1, e2)


def moba_attention(z_moba):
    t = z_moba.shape[0]
    nb = t // MOBA_BLOCK
    h, d = MOBA_HEADS, MOBA_HEAD_DIM
    kmean = moba_block_means(z_moba).reshape(nb, MOBA_WIDTH)
    return pl.pallas_call(
        _moba_kernel,
        grid=(h, nb),
        in_specs=[pl.BlockSpec((MOBA_BLOCK, d), lambda hh, i: (i, hh)),
                  pl.BlockSpec((t, d), lambda hh, i: (0, h + hh)),
                  pl.BlockSpec((t, d), lambda hh, i: (0, 2 * h + hh)),
                  pl.BlockSpec((nb, d), lambda hh, i: (0, hh))],
        out_specs=pl.BlockSpec((MOBA_BLOCK, d), lambda hh, i: (i, hh)),
        out_shape=jax.ShapeDtypeStruct((t, MOBA_WIDTH), BF16),
        compiler_params=_params(("parallel", "arbitrary")),
        name="moba_attention",
    )(z_moba, z_moba, z_moba, kmean)


def kernel(x, ln1_g, w_in, rwkv_mu, rwkv_w0, rwkv_w2, rwkv_a0, rwkv_a2, rwkv_g2, rwkv_k_k, rwkv_k_a,
           rwkv_r_k, rwkv_gn_w, rwkv_gn_b, w_proj_rwkv, w_proj_moba, w_out, ln2_g,
           peer_w_q, peer_sub_keys, peer_u, peer_v, ln_f_g):
    b, s, d = x.shape
    depth = w_in.shape[0]
    h = x.reshape(b * s, d)
    rkv = 3 * RWKV_WIDTH
    for l in range(depth):
        w_in_b = w_in[l].astype(BF16)
        xn = rmsnorm(h, ln1_g[l], BF16)
        z_rkv = matmul(xn, w_in_b[:, :rkv], F32, 1024, 1024, "in_proj_rwkv")
        z_l = matmul(xn, w_in_b[:, rkv:RWKV_COLS], F32, 1024, LORA_COLS, "in_proj_lora")
        z_moba = matmul(xn, w_in_b[:, RWKV_COLS:RWKV_COLS + MOBA_COLS], BF16, 1024, 1024, "in_proj_moba")
        z_gate = matmul(xn, w_in_b[:, RWKV_COLS + MOBA_COLS:], F32, 1024, 1024, "in_proj_gate")
        y_a = rwkv7(z_rkv, z_l, rwkv_mu[l], rwkv_w0[l], rwkv_w2[l], rwkv_a0[l], rwkv_a2[l], rwkv_g2[l],
                    rwkv_k_k[l], rwkv_k_a[l], rwkv_r_k[l], rwkv_gn_w[l], rwkv_gn_b[l])
        y_b = moba_attention(z_moba)
        merged = gated_merge(y_a, y_b, w_proj_rwkv[l].astype(BF16), w_proj_moba[l].astype(BF16), z_gate)
        h = matmul_residual(merged, w_out[l].astype(BF16), h)
        hn = rmsnorm(h, ln2_g[l], BF16)
        y_p = peer_ffn(hn, peer_w_q[l].astype(BF16), peer_sub_keys[l],
                       peer_u[l].astype(BF16), peer_v[l].astype(BF16))
        if l + 1 < depth:
            h = h + y_p
    return add_rmsnorm(h, y_p, ln_f_g).reshape(b, s, d)
```

```python
import functools

import jax
import jax.numpy as jnp
from jax import lax
from jax.experimental import pallas as pl
from jax.experimental.pallas import tpu as pltpu

F32 = jnp.float32
BF16 = jnp.bfloat16

D_MODEL = 4096
RMS_EPS = 1e-6

RWKV_HEADS = 32
RWKV_HEAD_DIM = 64
RWKV_WIDTH = RWKV_HEADS * RWKV_HEAD_DIM
DECAY_LORA = 64
ICLR_LORA = 64
GATE_LORA = 256
LORA_COLS = DECAY_LORA + ICLR_LORA + GATE_LORA
GN_EPS = 64e-5
RWKV_COLS = 3 * RWKV_WIDTH + LORA_COLS
CHUNK = 64
PAIR = 2 * RWKV_HEAD_DIM
N_PAIRS = RWKV_HEADS // 2

MOBA_HEADS = 16
MOBA_HEAD_DIM = 128
MOBA_WIDTH = MOBA_HEADS * MOBA_HEAD_DIM
MOBA_BLOCK = 256
MOBA_TOPK = 3
MOBA_COLS = 3 * MOBA_WIDTH

PEER_HEADS = 8
PEER_NKEYS = 128
PEER_EXPERTS = PEER_NKEYS * PEER_NKEYS
PEER_QDIM = 256
PEER_HALF = PEER_QDIM // 2
PEER_TOPK = 16

VMEM_LIMIT = 56 * 1024 * 1024

NEG_INF = float("-inf")


def _params(sem):
    return pltpu.CompilerParams(dimension_semantics=sem, vmem_limit_bytes=VMEM_LIMIT)


def _dot(a, b):
    return jnp.dot(a, b, preferred_element_type=F32)


def _dot_nt(a, b):
    return lax.dot_general(a, b, (((1,), (1,)), ((), ())), preferred_element_type=F32)


def _dot_tn(a, b):
    return lax.dot_general(a, b, (((0,), (0,)), ((), ())), preferred_element_type=F32)


def _split3(x):
    hi = x.astype(BF16)
    r1 = x - hi.astype(F32)
    mid = r1.astype(BF16)
    lo = (r1 - mid.astype(F32)).astype(BF16)
    return hi, mid, lo


def _dot_exact_rhs(x, e):
    hi, mid, lo = _split3(x)
    return _dot(hi, e) + _dot(mid, e) + _dot(lo, e)


def _dot_exact_lhs(e, x):
    hi, mid, lo = _split3(x)
    return _dot(e, hi) + _dot(e, mid) + _dot(e, lo)


def _rmsnorm_kernel(x_ref, g_ref, o_ref):
    x = x_ref[...]
    y = x * lax.rsqrt(jnp.mean(x * x, axis=-1, keepdims=True) + RMS_EPS)
    o_ref[...] = (y * g_ref[...]).astype(o_ref.dtype)


def rmsnorm(x, g, out_dtype, bm=256):
    t, d = x.shape
    return pl.pallas_call(
        _rmsnorm_kernel,
        grid=(t // bm,),
        in_specs=[pl.BlockSpec((bm, d), lambda i: (i, 0)), pl.BlockSpec((1, d), lambda i: (0, 0))],
        out_specs=pl.BlockSpec((bm, d), lambda i: (i, 0)),
        out_shape=jax.ShapeDtypeStruct((t, d), out_dtype),
        compiler_params=_params(("parallel",)),
        name="rmsnorm",
    )(x, g.reshape(1, d))


def _add_rmsnorm_kernel(a_ref, b_ref, g_ref, o_ref):
    x = a_ref[...] + b_ref[...]
    y = x * lax.rsqrt(jnp.mean(x * x, axis=-1, keepdims=True) + RMS_EPS)
    o_ref[...] = (y * g_ref[...]).astype(o_ref.dtype)


def add_rmsnorm(a, b, g, bm=256):
    t, d = a.shape
    return pl.pallas_call(
        _add_rmsnorm_kernel,
        grid=(t // bm,),
        in_specs=[pl.BlockSpec((bm, d), lambda i: (i, 0)), pl.BlockSpec((bm, d), lambda i: (i, 0)),
                  pl.BlockSpec((1, d), lambda i: (0, 0))],
        out_specs=pl.BlockSpec((bm, d), lambda i: (i, 0)),
        out_shape=jax.ShapeDtypeStruct((t, d), F32),
        compiler_params=_params(("parallel",)),
        name="add_rmsnorm",
    )(a, b, g.reshape(1, d))


def _mm_kernel(a_ref, b_ref, o_ref):
    o_ref[...] = _dot(a_ref[...], b_ref[...]).astype(o_ref.dtype)


def matmul(a, b, out_dtype, bm, bn, name):
    m, k = a.shape
    n = b.shape[1]
    bm, bn = min(bm, m), min(bn, n)
    return pl.pallas_call(
        _mm_kernel,
        grid=(n // bn, m // bm),
        in_specs=[pl.BlockSpec((bm, k), lambda j, i: (i, 0)), pl.BlockSpec((k, bn), lambda j, i: (0, j))],
        out_specs=pl.BlockSpec((bm, bn), lambda j, i: (i, j)),
        out_shape=jax.ShapeDtypeStruct((m, n), out_dtype),
        compiler_params=_params(("parallel", "parallel")),
        name=name,
    )(a, b)


def _merge_kernel(ya_ref, yb_ref, wa_ref, wb_ref, ga_ref, gb_ref, o_ref):
    pa = _dot(ya_ref[...], wa_ref[...])
    pb = _dot(yb_ref[...], wb_ref[...])
    o_ref[...] = (jax.nn.sigmoid(ga_ref[...]) * pa + jax.nn.sigmoid(gb_ref[...]) * pb).astype(o_ref.dtype)


def gated_merge(y_a, y_b, w_a, w_b, z_gate, bm=512, bn=1024):
    m, k = y_a.shape
    n = w_a.shape[1]
    bm = min(bm, m)
    nb = n // bn
    return pl.pallas_call(
        _merge_kernel,
        grid=(nb, m // bm),
        in_specs=[pl.BlockSpec((bm, k), lambda j, i: (i, 0)), pl.BlockSpec((bm, k), lambda j, i: (i, 0)),
                  pl.BlockSpec((k, bn), lambda j, i: (0, j)), pl.BlockSpec((k, bn), lambda j, i: (0, j)),
                  pl.BlockSpec((bm, bn), lambda j, i: (i, j)), pl.BlockSpec((bm, bn), lambda j, i: (i, j + nb))],
        out_specs=pl.BlockSpec((bm, bn), lambda j, i: (i, j)),
        out_shape=jax.ShapeDtypeStruct((m, n), BF16),
        compiler_params=_params(("parallel", "parallel")),
        name="gated_merge",
    )(y_a, y_b, w_a, w_b, z_gate, z_gate)


def _mm_residual_kernel(a_ref, b_ref, r_ref, o_ref):
    o_ref[...] = r_ref[...] + _dot(a_ref[...], b_ref[...])


def matmul_residual(a, b, res, bm=1024, bn=1024):
    m, k = a.shape
    n = b.shape[1]
    bm = min(bm, m)
    return pl.pallas_call(
        _mm_residual_kernel,
        grid=(n // bn, m // bm),
        in_specs=[pl.BlockSpec((bm, k), lambda j, i: (i, 0)), pl.BlockSpec((k, bn), lambda j, i: (0, j)),
                  pl.BlockSpec((bm, bn), lambda j, i: (i, j))],
        out_specs=pl.BlockSpec((bm, bn), lambda j, i: (i, j)),
        out_shape=jax.ShapeDtypeStruct((m, n), F32),
        compiler_params=_params(("parallel", "parallel")),
        name="out_proj_residual",
    )(a, b, res)


def _softplus(x):
    return jnp.maximum(x, 0.0) + jnp.log1p(jnp.exp(-jnp.abs(x)))


def _pair_masks(shape):
    lane = lax.broadcasted_iota(jnp.int32, shape, len(shape) - 1)
    return lane < RWKV_HEAD_DIM


def _block_ones():
    r = lax.broadcasted_iota(jnp.int32, (PAIR, PAIR), 0) // RWKV_HEAD_DIM
    c = lax.broadcasted_iota(jnp.int32, (PAIR, PAIR), 1) // RWKV_HEAD_DIM
    return jnp.where(r == c, 1.0, 0.0).astype(BF16)


def _rwkv_local_kernel(zr_ref, zk_ref, zv_ref, zl_ref, pr_ref, pk_ref, pv_ref, pl_ref,
                       mur_ref, muk_ref, muv_ref, mul_ref,
                       w0_ref, w2_ref, a0_ref, a2_ref, g2_ref, kk_ref, ka_ref, rk_ref,
                       ar_ref, vt_ref, y0_ref, arb_ref, bh_ref, kvt_ref, gam_ref, g_ref, bonus_ref):
    first = pl.program_id(1) == 0
    tm = zr_ref.shape[0]
    nc = tm // CHUNK

    def shifted(z_ref, p_ref, mu_ref):
        z = z_ref[...]
        prev_last = jnp.where(first, 0.0, p_ref[7:8, :])
        row = lax.broadcasted_iota(jnp.int32, z.shape, 0)
        zp = jnp.where(row == 0, prev_last, pltpu.roll(z, 1, axis=0))
        return z + (zp - z) * mu_ref[...]

    r = shifted(zr_ref, pr_ref, mur_ref)
    k = shifted(zk_ref, pk_ref, muk_ref)
    v = shifted(zv_ref, pv_ref, muv_ref)
    zl = shifted(zl_ref, pl_ref, mul_ref)
    wd = zl[:, :DECAY_LORA]
    ad = zl[:, DECAY_LORA:DECAY_LORA + ICLR_LORA]
    gd = zl[:, DECAY_LORA + ICLR_LORA:]

    wl = w0_ref[...] + _dot(jnp.tanh(wd).astype(BF16), w2_ref[...])
    logw = -jnp.exp(-_softplus(-wl) - 0.5)
    iclr = jax.nn.sigmoid(a0_ref[...] + _dot(ad.astype(BF16), a2_ref[...]))
    g_ref[...] = _dot(jax.nn.sigmoid(gd).astype(BF16), g2_ref[...])

    ones_blk = _block_ones()
    kk = k * kk_ref[...]
    kk = kk * lax.rsqrt(_dot_exact_rhs(kk * kk, ones_blk) + 1e-12)
    k = k * (1.0 + (iclr - 1.0) * ka_ref[...])
    beta = kk * iclr
    bonus_ref[...] = _dot_exact_rhs(r * k * rk_ref[...], ones_blk) * v

    ti = lax.broadcasted_iota(jnp.int32, (CHUNK, CHUNK), 0)
    si = lax.broadcasted_iota(jnp.int32, (CHUNK, CHUNK), 1)
    strict = ti > si
    incl = ti >= si
    eye = jnp.where(ti == si, 1.0, 0.0)
    head0 = _pair_masks((CHUNK, PAIR))
    head0_x = _pair_masks((2 * CHUNK, PAIR))
    vr = lax.broadcasted_iota(jnp.int32, (PAIR, PAIR), 0) // RWKV_HEAD_DIM
    vc = lax.broadcasted_iota(jnp.int32, (PAIR, PAIR), 1) // RWKV_HEAD_DIM
    same_head = vr == vc

    tt = lax.broadcasted_iota(jnp.int32, (tm, tm), 0)
    ts = lax.broadcasted_iota(jnp.int32, (tm, tm), 1)
    same_chunk = (tt // CHUNK) == (ts // CHUNK)
    cs = _dot_exact_lhs(jnp.where(same_chunk & (tt >= ts), 1.0, 0.0).astype(BF16), logw)
    ctot = _dot_exact_lhs(jnp.where(same_chunk, 1.0, 0.0).astype(BF16), logw)
    e_neg = jnp.exp(-cs)
    e_hat = jnp.exp(ctot - cs)
    abar = -kk * jnp.exp(cs - logw)
    rbar = r * jnp.exp(cs)
    bbar = (beta * e_neg).astype(BF16)
    kbar = (k * e_neg).astype(BF16)
    vb = v.astype(BF16)
    abar_b = abar.astype(BF16)
    bh_ref[...] = (beta * e_hat).astype(BF16)
    khat = (k * e_hat).astype(BF16)
    gam = jnp.exp(ctot)

    chunks = [slice(c * CHUNK, (c + 1) * CHUNK) for c in range(nc)]
    probs = [(c, h) for c in range(nc) for h in range(2)]
    xh = {}
    for c, rows in enumerate(chunks):
        x = jnp.concatenate([abar[rows], rbar[rows]], axis=0)
        xh[c, 0] = jnp.where(head0_x, x, 0.0).astype(BF16)
        xh[c, 1] = jnp.where(head0_x, 0.0, x).astype(BF16)
    mb = {q: _dot_nt(xh[q], bbar[chunks[q[0]]]) for q in probs}
    mk = {q: _dot_nt(xh[q], kbar[chunks[q[0]]]) for q in probs}
    npow = {q: jnp.where(strict, mb[q][:CHUNK], 0.0) for q in probs}
    a_ak = {q: jnp.where(strict, mk[q][:CHUNK], 0.0).astype(BF16) for q in probs}
    a_rb = {q: jnp.where(incl, mb[q][CHUNK:], 0.0) for q in probs}
    a_rk = {q: jnp.where(incl, mk[q][CHUNK:], 0.0).astype(BF16) for q in probs}
    tinv = {q: eye + npow[q] for q in probs}
    for _ in range(5):
        nb = {q: npow[q].astype(BF16) for q in probs}
        npow = {q: _dot(nb[q], nb[q]) for q in probs}
        tinv = {q: tinv[q] + _dot(npow[q].astype(BF16), tinv[q].astype(BF16)) for q in probs}
    tb = {q: tinv[q].astype(BF16) for q in probs}
    at = {q: _dot(tb[q], abar_b[chunks[q[0]]]) for q in probs}
    wv = {q: _dot(a_ak[q], vb[chunks[q[0]]]).astype(BF16) for q in probs}
    y0 = {q: _dot(a_rk[q], vb[chunks[q[0]]]) for q in probs}
    vt = {q: _dot(tb[q], wv[q]) for q in probs}
    kvt = [_dot_tn(vb[rows], khat[rows]) for rows in chunks]

    for c, rows in enumerate(chunks):
        atil = jnp.where(head0, at[c, 0], at[c, 1])
        ar_ref[c] = jnp.concatenate([atil, rbar[rows]], axis=0).astype(BF16)
        vt_ref[rows, :] = jnp.where(head0, vt[c, 0], vt[c, 1])
        y0_ref[rows, :] = jnp.where(head0, y0[c, 0], y0[c, 1])
        arb_ref[rows, :] = jnp.concatenate([a_rb[c, 0], a_rb[c, 1]], axis=1).astype(BF16)
        kvt_ref[c] = jnp.where(same_head, kvt[c], 0.0)
        gam_ref[c] = gam[c * CHUNK:c * CHUNK + 1, :]


def rwkv_local(z_rkv, z_l, mu, w0, w2, a0, a2, g2, k_k, k_a, r_k, tm=256):
    t = z_rkv.shape[0]
    nc = tm // CHUNK
    n_chunks = t // CHUNK
    w = RWKV_WIDTH
    mu_rkv = mu[:3 * w].reshape(1, 3 * w)
    mu_l = mu[3 * w:].reshape(1, LORA_COLS)
    rb = tm // 8

    def zspec(off):
        return pl.BlockSpec((tm, PAIR), lambda p, i: (i, p + off))

    def pspec(off):
        return pl.BlockSpec((8, PAIR), lambda p, i: (jnp.maximum(i * rb - 1, 0), p + off))

    def vspec(off=0):
        return pl.BlockSpec((1, PAIR), lambda p, i: (0, p + off))

    row = lambda a: a.reshape(1, w)
    tok = pl.BlockSpec((tm, PAIR), lambda p, i: (i, p))
    chk = pl.BlockSpec((nc, PAIR, PAIR), lambda p, i: (i, 0, p))
    f32 = lambda s: jax.ShapeDtypeStruct(s, F32)
    b16 = lambda s: jax.ShapeDtypeStruct(s, BF16)
    return pl.pallas_call(
        _rwkv_local_kernel,
        grid=(N_PAIRS, t // tm),
        in_specs=[zspec(0), zspec(N_PAIRS), zspec(2 * N_PAIRS),
                  pl.BlockSpec((tm, LORA_COLS), lambda p, i: (i, 0)),
                  pspec(0), pspec(N_PAIRS), pspec(2 * N_PAIRS),
                  pl.BlockSpec((8, LORA_COLS), lambda p, i: (jnp.maximum(i * rb - 1, 0), 0)),
                  vspec(0), vspec(N_PAIRS), vspec(2 * N_PAIRS),
                  pl.BlockSpec((1, LORA_COLS), lambda p, i: (0, 0)),
                  vspec(), pl.BlockSpec((DECAY_LORA, PAIR), lambda p, i: (0, p)),
                  vspec(), pl.BlockSpec((ICLR_LORA, PAIR), lambda p, i: (0, p)),
                  pl.BlockSpec((GATE_LORA, PAIR), lambda p, i: (0, p)),
                  vspec(), vspec(), vspec()],
        out_specs=[chk, tok, tok, tok, tok, chk,
                   pl.BlockSpec((nc, 1, PAIR), lambda p, i: (i, 0, p)), tok, tok],
        out_shape=[b16((n_chunks, PAIR, w)), f32((t, w)), f32((t, w)), b16((t, w)), b16((t, w)),
                   f32((n_chunks, PAIR, w)), f32((n_chunks, 1, w)), f32((t, w)), f32((t, w))],
        compiler_params=_params(("parallel", "parallel")),
        name="rwkv_local",
    )(z_rkv, z_rkv, z_rkv, z_l, z_rkv, z_rkv, z_rkv, z_l,
      mu_rkv, mu_rkv, mu_rkv, mu_l,
      row(w0), w2.astype(BF16), row(a0), a2.astype(BF16), g2.astype(BF16), row(k_k), row(k_a), row(r_k))


def _rwkv_scan_kernel(ar_ref, vt_ref, y0_ref, arb_ref, bh_ref, kvt_ref, gam_ref, g_ref, bonus_ref,
                      gnw_ref, gnb_ref, y_ref, s_ref):
    @pl.when(pl.program_id(0) == 0)
    def _():
        s_ref[...] = jnp.zeros_like(s_ref)

    head0 = _pair_masks((CHUNK, PAIR))
    vr = lax.broadcasted_iota(jnp.int32, (PAIR, PAIR), 0) // RWKV_HEAD_DIM
    vc = lax.broadcasted_iota(jnp.int32, (PAIR, PAIR), 1) // RWKV_HEAD_DIM
    same_head = vr == vc
    ones_blk = _block_ones()
    inv_n = 1.0 / RWKV_HEAD_DIM

    pairs = range(N_PAIRS)
    lanes = [slice(p * PAIR, (p + 1) * PAIR) for p in pairs]
    s = [s_ref[p] for p in pairs]
    xh = [_dot_nt(ar_ref[0, :, lanes[p]], s[p].astype(BF16)) for p in pairs]
    ub = [(xh[p][:CHUNK] + vt_ref[:, lanes[p]]).astype(BF16) for p in pairs]
    yu0 = [_dot(arb_ref[:, p * PAIR:p * PAIR + CHUNK], ub[p]) for p in pairs]
    yu1 = [_dot(arb_ref[:, p * PAIR + CHUNK:(p + 1) * PAIR], ub[p]) for p in pairs]
    su = [_dot_tn(ub[p], bh_ref[:, lanes[p]]) for p in pairs]
    for p in pairs:
        s_ref[p] = s[p] * gam_ref[0, :, lanes[p]] + jnp.where(same_head, su[p], 0.0) + kvt_ref[0, :, lanes[p]]
    y = [xh[p][CHUNK:] + jnp.where(head0, yu0[p], yu1[p]) + y0_ref[:, lanes[p]] for p in pairs]
    mom = [_dot_exact_rhs(jnp.concatenate([y[p], y[p] * y[p]], axis=0), ones_blk) * inv_n for p in pairs]
    for p in pairs:
        mean = mom[p][:CHUNK]
        var = mom[p][CHUNK:] - mean * mean
        yn = (y[p] - mean) * lax.rsqrt(var + GN_EPS) * gnw_ref[:, lanes[p]] + gnb_ref[:, lanes[p]]
        y_ref[:, lanes[p]] = ((yn + bonus_ref[:, lanes[p]]) * g_ref[:, lanes[p]]).astype(y_ref.dtype)


def rwkv_scan(ar, vt, y0, arb, bh, kvt, gam, g, bonus, gn_w, gn_b):
    t, w = vt.shape
    n_chunks = t // CHUNK
    tok = pl.BlockSpec((CHUNK, w), lambda c: (c, 0))
    chk = pl.BlockSpec((1, PAIR, w), lambda c: (c, 0, 0))
    vec = pl.BlockSpec((1, w), lambda c: (0, 0))
    return pl.pallas_call(
        _rwkv_scan_kernel,
        grid=(n_chunks,),
        in_specs=[chk, tok, tok, tok, tok, chk, pl.BlockSpec((1, 1, w), lambda c: (c, 0, 0)), tok, tok, vec, vec],
        out_specs=tok,
        out_shape=jax.ShapeDtypeStruct((t, w), BF16),
        scratch_shapes=[pltpu.VMEM((N_PAIRS, PAIR, PAIR), F32)],
        compiler_params=_params(("arbitrary",)),
        name="rwkv_scan",
    )(ar, vt, y0, arb, bh, kvt, gam, g, bonus, gn_w.reshape(1, w), gn_b.reshape(1, w))


def rwkv7(z_rkv, z_l, mu, w0, w2, a0, a2, g2, k_k, k_a, r_k, gn_w, gn_b):
    ar, vt, y0, arb, bh, kvt, gam, g, bonus = rwkv_local(
        z_rkv, z_l, mu, w0, w2, a0, a2, g2, k_k, k_a, r_k.reshape(-1))
    return rwkv_scan(ar, vt, y0, arb, bh, kvt, gam, g, bonus, gn_w, gn_b)


def _block_mean_kernel(k_ref, o_ref):
    o_ref[0] = jnp.mean(k_ref[...].astype(F32), axis=0, keepdims=True)


def moba_block_means(z_moba):
    t = z_moba.shape[0]
    nb = t // MOBA_BLOCK
    return pl.pallas_call(
        _block_mean_kernel,
        grid=(nb,),
        in_specs=[pl.BlockSpec((MOBA_BLOCK, MOBA_WIDTH), lambda i: (i, 1))],
        out_specs=pl.BlockSpec((1, 1, MOBA_WIDTH), lambda i: (i, 0, 0)),
        out_shape=jax.ShapeDtypeStruct((nb, 1, MOBA_WIDTH), F32),
        compiler_params=_params(("parallel",)),
        name="moba_block_means",
    )(z_moba)


def _transpose_v_kernel(v_ref, o_ref):
    for h in range(MOBA_HEADS):
        vh = v_ref[:, h * MOBA_HEAD_DIM:(h + 1) * MOBA_HEAD_DIM].astype(F32)
        o_ref[h, 0] = vh.T.astype(o_ref.dtype)


def moba_transpose_v(z_moba):
    t = z_moba.shape[0]
    nb = t // MOBA_BLOCK
    return pl.pallas_call(
        _transpose_v_kernel,
        grid=(nb,),
        in_specs=[pl.BlockSpec((MOBA_BLOCK, MOBA_WIDTH), lambda i: (i, 2))],
        out_specs=pl.BlockSpec((MOBA_HEADS, 1, MOBA_HEAD_DIM, MOBA_BLOCK), lambda i: (0, i, 0, 0)),
        out_shape=jax.ShapeDtypeStruct((MOBA_HEADS, nb, MOBA_HEAD_DIM, MOBA_BLOCK), z_moba.dtype),
        compiler_params=_params(("parallel",)),
        name="moba_transpose_v",
    )(z_moba)


MOBA_HEADS_PER_STEP = 4


def _moba_kernel(q_ref, k_ref, vt_ref, km_ref, o_ref, sel_ref):
    i = pl.program_id(1)
    bs, hd = MOBA_BLOCK, MOBA_HEAD_DIM
    nb = km_ref.shape[0]
    heads = range(MOBA_HEADS_PER_STEP)
    scale = hd ** -0.5
    q = [q_ref[:, g * hd:(g + 1) * hd] for g in heads]

    blk = lax.broadcasted_iota(jnp.int32, (nb, bs), 0)
    for g in heads:
        km_hi, km_mid, km_lo = _split3(km_ref[:, g * hd:(g + 1) * hd])
        gate = _dot_nt(km_hi, q[g]) + _dot_nt(km_mid, q[g]) + _dot_nt(km_lo, q[g])
        gate = jnp.where(blk < i, gate, NEG_INF)
        sel = jnp.zeros((nb, bs), F32)
        for _ in range(MOBA_TOPK):
            m = jnp.max(gate, axis=0, keepdims=True)
            cand = jnp.where((gate == m) & (m > NEG_INF), blk, nb)
            pick = blk == jnp.min(cand, axis=0, keepdims=True)
            sel = jnp.where(pick, 1.0, sel)
            gate = jnp.where(pick, NEG_INF, gate)
        sel_ref[g] = sel

    def keys(n, g):
        return k_ref[pl.ds(pl.multiple_of(n * bs, bs), bs), g * hd:(g + 1) * hd]

    ki = lax.broadcasted_iota(jnp.int32, (bs, bs), 0)
    qi = lax.broadcasted_iota(jnp.int32, (bs, bs), 1)
    s = [jnp.where(ki <= qi, _dot_nt(keys(i, g), q[g]) * scale, NEG_INF) for g in heads]
    m0 = [jnp.max(s[g], axis=0, keepdims=True) for g in heads]
    p = [jnp.exp(s[g] - m0[g]) for g in heads]
    l0 = [jnp.sum(p[g], axis=0, keepdims=True) for g in heads]
    acc0 = [_dot(vt_ref[g, i], p[g].astype(BF16)) for g in heads]

    def body(j, carry):
        m, l, acc = carry
        na = 2 * j
        nb_ok = na + 1 < i
        nbk = jnp.where(nb_ok, na + 1, na)
        sa = [_dot_nt(keys(na, g), q[g]) for g in heads]
        sb = [_dot_nt(keys(nbk, g), q[g]) for g in heads]
        m_new, l_new, pa, pb = [], [], [], []
        for g in heads:
            pick_a = sel_ref[g, pl.ds(na, 1), :] > 0.0
            pick_b = (sel_ref[g, pl.ds(nbk, 1), :] > 0.0) & nb_ok
            xa = jnp.where(pick_a, sa[g] * scale, NEG_INF)
            xb = jnp.where(pick_b, sb[g] * scale, NEG_INF)
            mg = jnp.maximum(m[g], jnp.maximum(jnp.max(xa, axis=0, keepdims=True),
                                               jnp.max(xb, axis=0, keepdims=True)))
            ea = jnp.exp(xa - mg)
            eb = jnp.exp(xb - mg)
            alpha = jnp.exp(m[g] - mg)
            m_new.append(mg)
            l_new.append(alpha * l[g] + jnp.sum(ea, axis=0, keepdims=True) + jnp.sum(eb, axis=0, keepdims=True))
            pa.append((alpha, ea.astype(BF16)))
            pb.append(eb.astype(BF16))
        acc_new = [pa[g][0] * acc[g] + _dot(vt_ref[g, na], pa[g][1]) + _dot(vt_ref[g, nbk], pb[g])
                   for g in heads]
        return tuple(m_new), tuple(l_new), tuple(acc_new)

    _, l, acc = lax.fori_loop(0, (i + 1) // 2, body, (tuple(m0), tuple(l0), tuple(acc0)))
    for g in heads:
        o_ref[:, g * hd:(g + 1) * hd] = (acc[g] / l[g]).T.astype(o_ref.dtype)


def _staircase_pairs():
    k = PEER_TOPK
    return [(a, b) for a in range(k) for b in range(k) if (a + 1) * (b + 1) <= k]


def _peer_route_kernel(q_ref, keys_ref, r2_ref, lb_ref, e1_ref, e2_ref, s_scr, rank_scr):
    tm = q_ref.shape[0]
    nk, k = PEER_NKEYS, PEER_TOPK
    rowi = lax.broadcasted_iota(jnp.int32, (nk, tm), 0)
    sub8 = lax.broadcasted_iota(jnp.int32, (PEER_HEADS, tm), 0)

    vals = [[jnp.zeros((PEER_HEADS, tm), F32) for _ in range(k)] for _ in range(2)]
    for hc in range(2 * PEER_HEADS):
        h, c = divmod(hc, 2)
        q_hi, q_mid, q_lo = _split3(q_ref[:, hc * PEER_HALF:(hc + 1) * PEER_HALF])
        kh = keys_ref[hc]
        s = _dot_nt(kh, q_hi) + _dot_nt(kh, q_mid) + _dot_nt(kh, q_lo)
        s_scr[hc] = s
        rank = jnp.full((nk, tm), float(k), F32)
        for j in range(k):
            m = jnp.max(s, axis=0, keepdims=True)
            idx = jnp.min(jnp.where(s == m, rowi, nk), axis=0, keepdims=True)
            pick = rowi == idx
            rank = jnp.where(pick, float(j), rank)
            s = jnp.where(pick, NEG_INF, s)
            vals[c][j] = jnp.where(sub8 == h, jnp.broadcast_to(m, (PEER_HEADS, tm)), vals[c][j])
        rank_scr[hc] = rank

    pairs = _staircase_pairs()
    csum = {p: vals[0][p[0]] + vals[1][p[1]] for p in pairs}
    cnt = {}
    for y in pairs:
        cnt[y] = jnp.full((PEER_HEADS, tm), float((y[0] + 1) * (y[1] + 1) - 1), F32)
    for xi, x in enumerate(pairs):
        for y in pairs[xi + 1:]:
            if x[0] <= y[0] and x[1] <= y[1]:
                continue
            x_first = jnp.where(csum[x] >= csum[y], 1.0, 0.0)
            cnt[y] = cnt[y] + x_first
            cnt[x] = cnt[x] + (1.0 - x_first)
    ex = [[jnp.exp(vals[c][j] - vals[c][0]) for j in range(k)] for c in range(2)]
    length = [jnp.zeros((PEER_HEADS, tm), F32) for _ in range(k)]
    z = jnp.zeros((PEER_HEADS, tm), F32)
    for p in pairs:
        chosen = jnp.where(cnt[p] < float(k), 1.0, 0.0)
        length[p[0]] = length[p[0]] + chosen
        z = z + chosen * (ex[0][p[0]] * ex[1][p[1]])
    inv_z = 1.0 / z

    for h in range(PEER_HEADS):
        r1 = rank_scr[2 * h]
        r2 = rank_scr[2 * h + 1]
        lb = jnp.zeros((nk, tm), F32)
        for a in range(k):
            lb = jnp.where(r1 == float(a), length[a][h:h + 1, :], lb)
        lb_ref[h] = lb
        e1_ref[h] = jnp.where(r1 < float(k), jnp.exp(s_scr[2 * h] - vals[0][0][h:h + 1, :]), 0.0)
        e2 = jnp.exp(s_scr[2 * h + 1] - vals[1][0][h:h + 1, :]) * inv_z[h:h + 1, :]
        e2_ref[h] = jnp.where(r2 < float(k), e2, 0.0).astype(e2_ref.dtype)
        r2_ref[h] = r2.astype(r2_ref.dtype)


def peer_route(qry, sub_keys, tm=256):
    t = qry.shape[0]
    tm = min(tm, t)
    nh, nk = PEER_HEADS, PEER_NKEYS
    keys = sub_keys.reshape(2 * nh, nk, PEER_HALF).astype(BF16)
    out = pl.BlockSpec((nh, nk, tm), lambda i: (0, 0, i))
    return pl.pallas_call(
        _peer_route_kernel,
        grid=(t // tm,),
        in_specs=[pl.BlockSpec((tm, 2 * nh * PEER_HALF), lambda i: (i, 0)),
                  pl.BlockSpec((2 * nh, nk, PEER_HALF), lambda i: (0, 0, 0))],
        out_specs=[out, out, out, out],
        out_shape=[jax.ShapeDtypeStruct((nh, nk, t), BF16), jax.ShapeDtypeStruct((nh, nk, t), F32),
                   jax.ShapeDtypeStruct((nh, nk, t), F32), jax.ShapeDtypeStruct((nh, nk, t), BF16)],
        scratch_shapes=[pltpu.VMEM((2 * nh, nk, tm), F32), pltpu.VMEM((2 * nh, nk, tm), F32)],
        compiler_params=_params(("parallel",)),
        name="peer_route",
    )(qry, keys)


PEER_I1_PER_TILE = 4


def _peer_expert_kernel(x_ref, u_ref, v_ref, r2_ref, lb_ref, e1_ref, e2_ref, o_ref):
    j = pl.program_id(1)
    tm = x_ref.shape[0]
    nk = PEER_NKEYS

    @pl.when(j == 0)
    def _():
        o_ref[...] = jnp.zeros_like(o_ref)

    hid = _dot_nt(u_ref[...], x_ref[...])
    act = 0.5 * hid * (1.0 + lax.erf(hid * (2.0 ** -0.5)))
    base = (j % (8 // PEER_I1_PER_TILE)) * PEER_I1_PER_TILE
    parts = []
    for a in range(PEER_I1_PER_TILE):
        g = jnp.zeros((nk, tm), BF16)
        for h in range(PEER_HEADS):
            lrow = lb_ref[h, pl.ds(base + a, 1), :]
            erow = e1_ref[h, pl.ds(base + a, 1), :]
            lbc = jnp.broadcast_to(lrow, (nk, tm)).astype(BF16)
            e1c = jnp.broadcast_to(erow, (nk, tm)).astype(BF16)
            g = g + jnp.where(r2_ref[h] < lbc, e1c * e2_ref[h], jnp.zeros((nk, tm), BF16))
        parts.append(g)
    gate = jnp.concatenate(parts, axis=0)
    p = (act * gate.astype(F32)).astype(BF16)
    o_ref[...] += _dot_tn(p, v_ref[...])


def peer_experts(hn, u_tab, v_tab, r2, lb, e1, e2, tm=512):
    t, d = hn.shape
    tm = min(tm, t)
    te = PEER_I1_PER_TILE * PEER_NKEYS
    nh, nk = PEER_HEADS, PEER_NKEYS
    full = pl.BlockSpec((nh, nk, tm), lambda i, j: (0, 0, i))
    rows = pl.BlockSpec((nh, 8, tm), lambda i, j: (0, j // (8 // PEER_I1_PER_TILE), i))
    return pl.pallas_call(
        _peer_expert_kernel,
        grid=(t // tm, PEER_EXPERTS // te),
        in_specs=[pl.BlockSpec((tm, d), lambda i, j: (i, 0)),
                  pl.BlockSpec((te, d), lambda i, j: (j, 0)),
                  pl.BlockSpec((te, d), lambda i, j: (j, 0)),
                  full, rows, rows, full],
        out_specs=pl.BlockSpec((tm, d), lambda i, j: (i, 0)),
        out_shape=jax.ShapeDtypeStruct((t, d), F32),
        compiler_params=_params(("parallel", "arbitrary")),
        name="peer_experts",
    )(hn, u_tab, v_tab, r2, lb, e1, e2)


def peer_ffn(hn, w_q, sub_keys, u_tab, v_tab):
    qry = matmul(hn, w_q, F32, 1024, 1024, "peer_query")
    r2, lb, e1, e2 = peer_route(qry, sub_keys)
    return peer_experts(hn, u_tab, v_tab, r2, lb, e1, e2)


def moba_attention(z_moba):
    t = z_moba.shape[0]
    nb = t // MOBA_BLOCK
    g = MOBA_HEADS_PER_STEP
    groups = MOBA_HEADS // g
    gw = g * MOBA_HEAD_DIM
    kmean = moba_block_means(z_moba).reshape(nb, MOBA_WIDTH)
    v_t = moba_transpose_v(z_moba)
    return pl.pallas_call(
        _moba_kernel,
        grid=(groups, nb),
        in_specs=[pl.BlockSpec((MOBA_BLOCK, gw), lambda hg, i: (i, hg)),
                  pl.BlockSpec((t, gw), lambda hg, i: (0, groups + hg)),
                  pl.BlockSpec((g, nb, MOBA_HEAD_DIM, MOBA_BLOCK), lambda hg, i: (hg, 0, 0, 0)),
                  pl.BlockSpec((nb, gw), lambda hg, i: (0, hg))],
        out_specs=pl.BlockSpec((MOBA_BLOCK, gw), lambda hg, i: (i, hg)),
        out_shape=jax.ShapeDtypeStruct((t, MOBA_WIDTH), BF16),
        scratch_shapes=[pltpu.VMEM((g, nb, MOBA_BLOCK), F32)],
        compiler_params=_params(("parallel", "arbitrary")),
        name="moba_attention",
    )(z_moba, z_moba, v_t, kmean)


def kernel(x, ln1_g, w_in, rwkv_mu, rwkv_w0, rwkv_w2, rwkv_a0, rwkv_a2, rwkv_g2, rwkv_k_k, rwkv_k_a,
           rwkv_r_k, rwkv_gn_w, rwkv_gn_b, w_proj_rwkv, w_proj_moba, w_out, ln2_g,
           peer_w_q, peer_sub_keys, peer_u, peer_v, ln_f_g):
    b, s, d = x.shape
    depth = w_in.shape[0]
    h = x.reshape(b * s, d)
    rkv = 3 * RWKV_WIDTH
    for l in range(depth):
        w_in_b = w_in[l].astype(BF16)
        xn = rmsnorm(h, ln1_g[l], BF16)
        z_rkv = matmul(xn, w_in_b[:, :rkv], F32, 1024, 1024, "in_proj_rwkv")
        z_l = matmul(xn, w_in_b[:, rkv:RWKV_COLS], F32, 1024, LORA_COLS, "in_proj_lora")
        z_moba = matmul(xn, w_in_b[:, RWKV_COLS:RWKV_COLS + MOBA_COLS], BF16, 1024, 1024, "in_proj_moba")
        z_gate = matmul(xn, w_in_b[:, RWKV_COLS + MOBA_COLS:], F32, 1024, 1024, "in_proj_gate")
        y_a = rwkv7(z_rkv, z_l, rwkv_mu[l], rwkv_w0[l], rwkv_w2[l], rwkv_a0[l], rwkv_a2[l], rwkv_g2[l],
                    rwkv_k_k[l], rwkv_k_a[l], rwkv_r_k[l], rwkv_gn_w[l], rwkv_gn_b[l])
        y_b = moba_attention(z_moba)
        merged = gated_merge(y_a, y_b, w_proj_rwkv[l].astype(BF16), w_proj_moba[l].astype(BF16), z_gate)
        h = matmul_residual(merged, w_out[l].astype(BF16), h)
        hn = rmsnorm(h, ln2_g[l], BF16)
        y_p = peer_ffn(hn, peer_w_q[l].astype(BF16), peer_sub_keys[l],
                       peer_u[l].astype(BF16), peer_v[l].astype(BF16))
        if l + 1 < depth:
            h = h + y_p
    return add_rmsnorm(h, y_p, ln_f_g).reshape(b, s, d)
```

```python
import functools

import jax
import jax.numpy as jnp
from jax import lax
from jax.experimental import pallas as pl
from jax.experimental.pallas import tpu as pltpu

F32 = jnp.float32
BF16 = jnp.bfloat16

D_MODEL = 4096
RMS_EPS = 1e-6

RWKV_HEADS = 32
RWKV_HEAD_DIM = 64
RWKV_WIDTH = RWKV_HEADS * RWKV_HEAD_DIM
DECAY_LORA = 64
ICLR_LORA = 64
GATE_LORA = 256
LORA_COLS = DECAY_LORA + ICLR_LORA + GATE_LORA
GN_EPS = 64e-5
RWKV_COLS = 3 * RWKV_WIDTH + LORA_COLS
CHUNK = 64
PAIR = 2 * RWKV_HEAD_DIM
N_PAIRS = RWKV_HEADS // 2

MOBA_HEADS = 16
MOBA_HEAD_DIM = 128
MOBA_WIDTH = MOBA_HEADS * MOBA_HEAD_DIM
MOBA_BLOCK = 256
MOBA_TOPK = 3
MOBA_COLS = 3 * MOBA_WIDTH

PEER_HEADS = 8
PEER_NKEYS = 128
PEER_EXPERTS = PEER_NKEYS * PEER_NKEYS
PEER_QDIM = 256
PEER_HALF = PEER_QDIM // 2
PEER_TOPK = 16

VMEM_LIMIT = 56 * 1024 * 1024

NEG_INF = float("-inf")


def _params(sem):
    return pltpu.CompilerParams(dimension_semantics=sem, vmem_limit_bytes=VMEM_LIMIT)


def _dot(a, b):
    return jnp.dot(a, b, preferred_element_type=F32)


def _dot_nt(a, b):
    return lax.dot_general(a, b, (((1,), (1,)), ((), ())), preferred_element_type=F32)


def _dot_tn(a, b):
    return lax.dot_general(a, b, (((0,), (0,)), ((), ())), preferred_element_type=F32)


def _split(x, pieces):
    out = []
    for _ in range(pieces - 1):
        hi = x.astype(BF16)
        out.append(hi)
        x = x - hi.astype(F32)
    out.append(x.astype(BF16))
    return out


def _split3(x):
    return _split(x, 3)


def _dot_exact_rhs(x, e, pieces=3):
    return sum(_dot(p, e) for p in _split(x, pieces))


def _dot_exact_lhs(e, x, pieces=3):
    return sum(_dot(e, p) for p in _split(x, pieces))


def _rmsnorm_kernel(x_ref, g_ref, o_ref):
    x = x_ref[...]
    y = x * lax.rsqrt(jnp.mean(x * x, axis=-1, keepdims=True) + RMS_EPS)
    o_ref[...] = (y * g_ref[...]).astype(o_ref.dtype)


def rmsnorm(x, g, out_dtype, bm=256):
    t, d = x.shape
    return pl.pallas_call(
        _rmsnorm_kernel,
        grid=(t // bm,),
        in_specs=[pl.BlockSpec((bm, d), lambda i: (i, 0)), pl.BlockSpec((1, d), lambda i: (0, 0))],
        out_specs=pl.BlockSpec((bm, d), lambda i: (i, 0)),
        out_shape=jax.ShapeDtypeStruct((t, d), out_dtype),
        compiler_params=_params(("parallel",)),
        name="rmsnorm",
    )(x, g.reshape(1, d))


def _add_rmsnorm_kernel(a_ref, b_ref, g_ref, o_ref):
    x = a_ref[...] + b_ref[...]
    y = x * lax.rsqrt(jnp.mean(x * x, axis=-1, keepdims=True) + RMS_EPS)
    o_ref[...] = (y * g_ref[...]).astype(o_ref.dtype)


def add_rmsnorm(a, b, g, bm=256):
    t, d = a.shape
    return pl.pallas_call(
        _add_rmsnorm_kernel,
        grid=(t // bm,),
        in_specs=[pl.BlockSpec((bm, d), lambda i: (i, 0)), pl.BlockSpec((bm, d), lambda i: (i, 0)),
                  pl.BlockSpec((1, d), lambda i: (0, 0))],
        out_specs=pl.BlockSpec((bm, d), lambda i: (i, 0)),
        out_shape=jax.ShapeDtypeStruct((t, d), F32),
        compiler_params=_params(("parallel",)),
        name="add_rmsnorm",
    )(a, b, g.reshape(1, d))


def _mm_kernel(a_ref, b_ref, o_ref):
    o_ref[...] = _dot(a_ref[...], b_ref[...]).astype(o_ref.dtype)


def _mm_sigmoid_kernel(a_ref, b_ref, o_ref):
    o_ref[...] = jax.nn.sigmoid(_dot(a_ref[...], b_ref[...])).astype(o_ref.dtype)


def matmul(a, b, out_dtype, bm, bn, name, sigmoid=False):
    m, k = a.shape
    n = b.shape[1]
    bm, bn = min(bm, m), min(bn, n)
    return pl.pallas_call(
        _mm_sigmoid_kernel if sigmoid else _mm_kernel,
        grid=(n // bn, m // bm),
        in_specs=[pl.BlockSpec((bm, k), lambda j, i: (i, 0)), pl.BlockSpec((k, bn), lambda j, i: (0, j))],
        out_specs=pl.BlockSpec((bm, bn), lambda j, i: (i, j)),
        out_shape=jax.ShapeDtypeStruct((m, n), out_dtype),
        compiler_params=_params(("parallel", "parallel")),
        name=name,
    )(a, b)


def _merge_kernel(ya_ref, yb_ref, wa_ref, wb_ref, ga_ref, gb_ref, o_ref):
    pa = _dot(ya_ref[...], wa_ref[...])
    pb = _dot(yb_ref[...], wb_ref[...])
    o_ref[...] = (ga_ref[...].astype(F32) * pa + gb_ref[...].astype(F32) * pb).astype(o_ref.dtype)


def gated_merge(y_a, y_b, w_a, w_b, z_gate, bm=1024, bn=1024):
    m, k = y_a.shape
    n = w_a.shape[1]
    bm = min(bm, m)
    nb = n // bn
    return pl.pallas_call(
        _merge_kernel,
        grid=(nb, m // bm),
        in_specs=[pl.BlockSpec((bm, k), lambda j, i: (i, 0)), pl.BlockSpec((bm, k), lambda j, i: (i, 0)),
                  pl.BlockSpec((k, bn), lambda j, i: (0, j)), pl.BlockSpec((k, bn), lambda j, i: (0, j)),
                  pl.BlockSpec((bm, bn), lambda j, i: (i, j)), pl.BlockSpec((bm, bn), lambda j, i: (i, j + nb))],
        out_specs=pl.BlockSpec((bm, bn), lambda j, i: (i, j)),
        out_shape=jax.ShapeDtypeStruct((m, n), BF16),
        compiler_params=_params(("parallel", "parallel")),
        name="gated_merge",
    )(y_a, y_b, w_a, w_b, z_gate, z_gate)


def _mm_residual_kernel(a_ref, b_ref, r_ref, o_ref):
    o_ref[...] = r_ref[...] + _dot(a_ref[...], b_ref[...])


def matmul_residual(a, b, res, bm=1024, bn=1024):
    m, k = a.shape
    n = b.shape[1]
    bm = min(bm, m)
    return pl.pallas_call(
        _mm_residual_kernel,
        grid=(n // bn, m // bm),
        in_specs=[pl.BlockSpec((bm, k), lambda j, i: (i, 0)), pl.BlockSpec((k, bn), lambda j, i: (0, j)),
                  pl.BlockSpec((bm, bn), lambda j, i: (i, j))],
        out_specs=pl.BlockSpec((bm, bn), lambda j, i: (i, j)),
        out_shape=jax.ShapeDtypeStruct((m, n), F32),
        compiler_params=_params(("parallel", "parallel")),
        name="out_proj_residual",
    )(a, b, res)


def _pair_masks(shape):
    lane = lax.broadcasted_iota(jnp.int32, shape, len(shape) - 1)
    return lane < RWKV_HEAD_DIM


def _block_ones():
    r = lax.broadcasted_iota(jnp.int32, (PAIR, PAIR), 0) // RWKV_HEAD_DIM
    c = lax.broadcasted_iota(jnp.int32, (PAIR, PAIR), 1) // RWKV_HEAD_DIM
    return jnp.where(r == c, 1.0, 0.0).astype(BF16)


def _rwkv_local_kernel(zr_ref, zk_ref, zv_ref, zl_ref, pr_ref, pk_ref, pv_ref, pl_ref,
                       mur_ref, muk_ref, muv_ref, mul_ref,
                       w0_ref, w2_ref, a0_ref, a2_ref, g2_ref, kk_ref, ka_ref, rk_ref,
                       ar_ref, vt_ref, y0_ref, arb_ref, bh_ref, kvt_ref, gam_ref, g_ref, bonus_ref):
    first = pl.program_id(1) == 0
    tm = zr_ref.shape[0]
    nc = tm // CHUNK

    def shifted(z_ref, p_ref, mu_ref):
        z = z_ref[...]
        prev_last = jnp.where(first, 0.0, p_ref[7:8, :])
        row = lax.broadcasted_iota(jnp.int32, z.shape, 0)
        zp = jnp.where(row == 0, prev_last, pltpu.roll(z, 1, axis=0))
        return z + (zp - z) * mu_ref[...]

    r = shifted(zr_ref, pr_ref, mur_ref)
    k = shifted(zk_ref, pk_ref, muk_ref)
    v = shifted(zv_ref, pv_ref, muv_ref)
    zl = shifted(zl_ref, pl_ref, mul_ref)
    wd = zl[:, :DECAY_LORA]
    ad = zl[:, DECAY_LORA:DECAY_LORA + ICLR_LORA]
    gd = zl[:, DECAY_LORA + ICLR_LORA:]

    wl = w0_ref[...] + _dot(jnp.tanh(wd).astype(BF16), w2_ref[...])
    logw = -(2.718281828459045 ** -0.5) * jax.nn.sigmoid(wl)
    iclr = jax.nn.sigmoid(a0_ref[...] + _dot(ad.astype(BF16), a2_ref[...]))
    g_ref[...] = _dot(jax.nn.sigmoid(gd).astype(BF16), g2_ref[...])

    ones_blk = _block_ones()
    kk = k * kk_ref[...]
    kk = kk * lax.rsqrt(_dot_exact_rhs(kk * kk, ones_blk, 2) + 1e-12)
    k = k * (1.0 + (iclr - 1.0) * ka_ref[...])
    beta = kk * iclr
    bonus_ref[...] = _dot_exact_rhs(r * k * rk_ref[...], ones_blk, 2) * v

    ti = lax.broadcasted_iota(jnp.int32, (CHUNK, CHUNK), 0)
    si = lax.broadcasted_iota(jnp.int32, (CHUNK, CHUNK), 1)
    strict = ti > si
    incl = ti >= si
    eye = jnp.where(ti == si, 1.0, 0.0)
    head0 = _pair_masks((CHUNK, PAIR))
    head0_x = _pair_masks((2 * CHUNK, PAIR))
    vr = lax.broadcasted_iota(jnp.int32, (PAIR, PAIR), 0) // RWKV_HEAD_DIM
    vc = lax.broadcasted_iota(jnp.int32, (PAIR, PAIR), 1) // RWKV_HEAD_DIM
    same_head = vr == vc

    tt = lax.broadcasted_iota(jnp.int32, (tm, tm), 0)
    ts = lax.broadcasted_iota(jnp.int32, (tm, tm), 1)
    same_chunk = (tt // CHUNK) == (ts // CHUNK)
    cs = _dot_exact_lhs(jnp.where(same_chunk & (tt >= ts), 1.0, 0.0).astype(BF16), logw, 2)
    ctot = jnp.concatenate([jnp.broadcast_to(cs[(c + 1) * CHUNK - 1:(c + 1) * CHUNK], (CHUNK, PAIR))
                            for c in range(nc)], axis=0)
    e_neg = jnp.exp(-cs)
    e_hat = jnp.exp(ctot - cs)
    abar = -kk * jnp.exp(cs - logw)
    rbar = r * jnp.exp(cs)
    bbar = (beta * e_neg).astype(BF16)
    kbar = (k * e_neg).astype(BF16)
    vb = v.astype(BF16)
    abar_b = abar.astype(BF16)
    bh_ref[...] = (beta * e_hat).astype(BF16)
    khat = (k * e_hat).astype(BF16)
    gam = jnp.exp(ctot)

    chunks = [slice(c * CHUNK, (c + 1) * CHUNK) for c in range(nc)]
    probs = [(c, h) for c in range(nc) for h in range(2)]
    xh = {}
    for c, rows in enumerate(chunks):
        x = jnp.concatenate([abar[rows], rbar[rows]], axis=0)
        xh[c, 0] = jnp.where(head0_x, x, 0.0).astype(BF16)
        xh[c, 1] = jnp.where(head0_x, 0.0, x).astype(BF16)
    mb = {q: _dot_nt(xh[q], bbar[chunks[q[0]]]) for q in probs}
    mk = {q: _dot_nt(xh[q], kbar[chunks[q[0]]]) for q in probs}
    npow = {q: jnp.where(strict, mb[q][:CHUNK], 0.0) for q in probs}
    a_ak = {q: jnp.where(strict, mk[q][:CHUNK], 0.0).astype(BF16) for q in probs}
    a_rb = {q: jnp.where(incl, mb[q][CHUNK:], 0.0) for q in probs}
    a_rk = {q: jnp.where(incl, mk[q][CHUNK:], 0.0).astype(BF16) for q in probs}
    tinv = {q: eye + npow[q] for q in probs}
    for _ in range(5):
        nb = {q: npow[q].astype(BF16) for q in probs}
        npow = {q: _dot(nb[q], nb[q]) for q in probs}
        tinv = {q: tinv[q] + _dot(npow[q].astype(BF16), tinv[q].astype(BF16)) for q in probs}
    tb = {q: tinv[q].astype(BF16) for q in probs}
    at = {q: _dot(tb[q], abar_b[chunks[q[0]]]) for q in probs}
    wv = {q: _dot(a_ak[q], vb[chunks[q[0]]]).astype(BF16) for q in probs}
    y0 = {q: _dot(a_rk[q], vb[chunks[q[0]]]) for q in probs}
    vt = {q: _dot(tb[q], wv[q]) for q in probs}
    kvt = [_dot_tn(vb[rows], khat[rows]) for rows in chunks]

    for c, rows in enumerate(chunks):
        atil = jnp.where(head0, at[c, 0], at[c, 1])
        ar_ref[c] = jnp.concatenate([atil, rbar[rows]], axis=0).astype(BF16)
        vt_ref[rows, :] = jnp.where(head0, vt[c, 0], vt[c, 1])
        y0_ref[rows, :] = jnp.where(head0, y0[c, 0], y0[c, 1])
        arb_ref[rows, :] = jnp.concatenate([a_rb[c, 0], a_rb[c, 1]], axis=1).astype(BF16)
        kvt_ref[c] = jnp.where(same_head, kvt[c], 0.0)
        gam_ref[c] = gam[c * CHUNK:c * CHUNK + 1, :]


def rwkv_local(z_rkv, z_l, mu, w0, w2, a0, a2, g2, k_k, k_a, r_k, tm=256):
    t = z_rkv.shape[0]
    nc = tm // CHUNK
    n_chunks = t // CHUNK
    w = RWKV_WIDTH
    mu_rkv = mu[:3 * w].reshape(1, 3 * w)
    mu_l = mu[3 * w:].reshape(1, LORA_COLS)
    rb = tm // 8

    def zspec(off):
        return pl.BlockSpec((tm, PAIR), lambda p, i: (i, p + off))

    def pspec(off):
        return pl.BlockSpec((8, PAIR), lambda p, i: (jnp.maximum(i * rb - 1, 0), p + off))

    def vspec(off=0):
        return pl.BlockSpec((1, PAIR), lambda p, i: (0, p + off))

    row = lambda a: a.reshape(1, w)
    tok = pl.BlockSpec((tm, PAIR), lambda p, i: (i, p))
    chk = pl.BlockSpec((nc, PAIR, PAIR), lambda p, i: (i, 0, p))
    f32 = lambda s: jax.ShapeDtypeStruct(s, F32)
    b16 = lambda s: jax.ShapeDtypeStruct(s, BF16)
    return pl.pallas_call(
        _rwkv_local_kernel,
        grid=(N_PAIRS, t // tm),
        in_specs=[zspec(0), zspec(N_PAIRS), zspec(2 * N_PAIRS),
                  pl.BlockSpec((tm, LORA_COLS), lambda p, i: (i, 0)),
                  pspec(0), pspec(N_PAIRS), pspec(2 * N_PAIRS),
                  pl.BlockSpec((8, LORA_COLS), lambda p, i: (jnp.maximum(i * rb - 1, 0), 0)),
                  vspec(0), vspec(N_PAIRS), vspec(2 * N_PAIRS),
                  pl.BlockSpec((1, LORA_COLS), lambda p, i: (0, 0)),
                  vspec(), pl.BlockSpec((DECAY_LORA, PAIR), lambda p, i: (0, p)),
                  vspec(), pl.BlockSpec((ICLR_LORA, PAIR), lambda p, i: (0, p)),
                  pl.BlockSpec((GATE_LORA, PAIR), lambda p, i: (0, p)),
                  vspec(), vspec(), vspec()],
        out_specs=[chk, tok, tok, tok, tok, chk,
                   pl.BlockSpec((nc, 1, PAIR), lambda p, i: (i, 0, p)), tok, tok],
        out_shape=[b16((n_chunks, PAIR, w)), f32((t, w)), f32((t, w)), b16((t, w)), b16((t, w)),
                   f32((n_chunks, PAIR, w)), f32((n_chunks, 1, w)), f32((t, w)), f32((t, w))],
        compiler_params=_params(("parallel", "parallel")),
        name="rwkv_local",
    )(z_rkv, z_rkv, z_rkv, z_l, z_rkv, z_rkv, z_rkv, z_l,
      mu_rkv, mu_rkv, mu_rkv, mu_l,
      row(w0), w2.astype(BF16), row(a0), a2.astype(BF16), g2.astype(BF16), row(k_k), row(k_a), row(r_k))


def _rwkv_scan_kernel(ar_ref, vt_ref, y0_ref, arb_ref, bh_ref, kvt_ref, gam_ref, g_ref, bonus_ref,
                      gnw_ref, gnb_ref, y_ref, s_ref):
    @pl.when(pl.program_id(0) == 0)
    def _():
        s_ref[...] = jnp.zeros_like(s_ref)

    head0 = _pair_masks((CHUNK, PAIR))
    vr = lax.broadcasted_iota(jnp.int32, (PAIR, PAIR), 0) // RWKV_HEAD_DIM
    vc = lax.broadcasted_iota(jnp.int32, (PAIR, PAIR), 1) // RWKV_HEAD_DIM
    same_head = vr == vc
    ones_blk = _block_ones()
    inv_n = 1.0 / RWKV_HEAD_DIM

    pairs = range(N_PAIRS)
    lanes = [slice(p * PAIR, (p + 1) * PAIR) for p in pairs]
    s = [s_ref[p] for p in pairs]
    xh = [_dot_nt(ar_ref[0, :, lanes[p]], s[p].astype(BF16)) for p in pairs]
    ub = [(xh[p][:CHUNK] + vt_ref[:, lanes[p]]).astype(BF16) for p in pairs]
    yu0 = [_dot(arb_ref[:, p * PAIR:p * PAIR + CHUNK], ub[p]) for p in pairs]
    yu1 = [_dot(arb_ref[:, p * PAIR + CHUNK:(p + 1) * PAIR], ub[p]) for p in pairs]
    su = [_dot_tn(ub[p], bh_ref[:, lanes[p]]) for p in pairs]
    for p in pairs:
        s_ref[p] = s[p] * gam_ref[0, :, lanes[p]] + jnp.where(same_head, su[p], 0.0) + kvt_ref[0, :, lanes[p]]
    y = [xh[p][CHUNK:] + jnp.where(head0, yu0[p], yu1[p]) + y0_ref[:, lanes[p]] for p in pairs]
    mom = [_dot_exact_rhs(jnp.concatenate([y[p], y[p] * y[p]], axis=0), ones_blk) * inv_n for p in pairs]
    for p in pairs:
        mean = mom[p][:CHUNK]
        var = mom[p][CHUNK:] - mean * mean
        yn = (y[p] - mean) * lax.rsqrt(var + GN_EPS) * gnw_ref[:, lanes[p]] + gnb_ref[:, lanes[p]]
        y_ref[:, lanes[p]] = ((yn + bonus_ref[:, lanes[p]]) * g_ref[:, lanes[p]]).astype(y_ref.dtype)


def rwkv_scan(ar, vt, y0, arb, bh, kvt, gam, g, bonus, gn_w, gn_b):
    t, w = vt.shape
    n_chunks = t // CHUNK
    tok = pl.BlockSpec((CHUNK, w), lambda c: (c, 0))
    chk = pl.BlockSpec((1, PAIR, w), lambda c: (c, 0, 0))
    vec = pl.BlockSpec((1, w), lambda c: (0, 0))
    return pl.pallas_call(
        _rwkv_scan_kernel,
        grid=(n_chunks,),
        in_specs=[chk, tok, tok, tok, tok, chk, pl.BlockSpec((1, 1, w), lambda c: (c, 0, 0)), tok, tok, vec, vec],
        out_specs=tok,
        out_shape=jax.ShapeDtypeStruct((t, w), BF16),
        scratch_shapes=[pltpu.VMEM((N_PAIRS, PAIR, PAIR), F32)],
        compiler_params=_params(("arbitrary",)),
        name="rwkv_scan",
    )(ar, vt, y0, arb, bh, kvt, gam, g, bonus, gn_w.reshape(1, w), gn_b.reshape(1, w))


def rwkv7(z_rkv, z_l, mu, w0, w2, a0, a2, g2, k_k, k_a, r_k, gn_w, gn_b):
    ar, vt, y0, arb, bh, kvt, gam, g, bonus = rwkv_local(
        z_rkv, z_l, mu, w0, w2, a0, a2, g2, k_k, k_a, r_k.reshape(-1))
    return rwkv_scan(ar, vt, y0, arb, bh, kvt, gam, g, bonus, gn_w, gn_b)


def _block_mean_kernel(k_ref, o_ref):
    o_ref[0] = jnp.mean(k_ref[...].astype(F32), axis=0, keepdims=True)


def moba_block_means(z_moba):
    t = z_moba.shape[0]
    nb = t // MOBA_BLOCK
    return pl.pallas_call(
        _block_mean_kernel,
        grid=(nb,),
        in_specs=[pl.BlockSpec((MOBA_BLOCK, MOBA_WIDTH), lambda i: (i, 1))],
        out_specs=pl.BlockSpec((1, 1, MOBA_WIDTH), lambda i: (i, 0, 0)),
        out_shape=jax.ShapeDtypeStruct((nb, 1, MOBA_WIDTH), F32),
        compiler_params=_params(("parallel",)),
        name="moba_block_means",
    )(z_moba)


def _transpose_v_kernel(v_ref, o_ref):
    for h in range(MOBA_HEADS):
        vh = v_ref[:, h * MOBA_HEAD_DIM:(h + 1) * MOBA_HEAD_DIM].astype(F32)
        o_ref[h, 0] = vh.T.astype(o_ref.dtype)


def moba_transpose_v(z_moba):
    t = z_moba.shape[0]
    nb = t // MOBA_BLOCK
    return pl.pallas_call(
        _transpose_v_kernel,
        grid=(nb,),
        in_specs=[pl.BlockSpec((MOBA_BLOCK, MOBA_WIDTH), lambda i: (i, 2))],
        out_specs=pl.BlockSpec((MOBA_HEADS, 1, MOBA_HEAD_DIM, MOBA_BLOCK), lambda i: (0, i, 0, 0)),
        out_shape=jax.ShapeDtypeStruct((MOBA_HEADS, nb, MOBA_HEAD_DIM, MOBA_BLOCK), z_moba.dtype),
        compiler_params=_params(("parallel",)),
        name="moba_transpose_v",
    )(z_moba)


MOBA_HEADS_PER_STEP = 4


def _moba_kernel(q_ref, k_ref, vt_ref, km_ref, o_ref, sel_ref):
    i = pl.program_id(1)
    bs, hd = MOBA_BLOCK, MOBA_HEAD_DIM
    nb = km_ref.shape[0]
    heads = range(MOBA_HEADS_PER_STEP)
    scale = hd ** -0.5 * 1.4426950408889634
    q = [q_ref[:, g * hd:(g + 1) * hd] for g in heads]

    blk = lax.broadcasted_iota(jnp.int32, (nb, bs), 0)
    for g in heads:
        km_hi, km_mid, km_lo = _split3(km_ref[:, g * hd:(g + 1) * hd])
        gate = _dot_nt(km_hi, q[g]) + _dot_nt(km_mid, q[g]) + _dot_nt(km_lo, q[g])
        gate = jnp.where(blk < i, gate, NEG_INF)
        sel = jnp.zeros((nb, bs), F32)
        for _ in range(MOBA_TOPK):
            m = jnp.max(gate, axis=0, keepdims=True)
            cand = jnp.where((gate == m) & (m > NEG_INF), blk, nb)
            pick = blk == jnp.min(cand, axis=0, keepdims=True)
            sel = jnp.where(pick, 1.0, sel)
            gate = jnp.where(pick, NEG_INF, gate)
        sel_ref[g] = sel

    def keys(n, g):
        return k_ref[pl.ds(pl.multiple_of(n * bs, bs), bs), g * hd:(g + 1) * hd]

    ki = lax.broadcasted_iota(jnp.int32, (bs, bs), 0)
    qi = lax.broadcasted_iota(jnp.int32, (bs, bs), 1)
    s = [jnp.where(ki <= qi, _dot_nt(keys(i, g), q[g]) * scale, NEG_INF) for g in heads]
    m0 = [jnp.max(s[g], axis=0, keepdims=True) for g in heads]
    p = [jnp.exp2(s[g] - m0[g]) for g in heads]
    l0 = [jnp.sum(p[g], axis=0, keepdims=True) for g in heads]
    acc0 = [_dot(vt_ref[g, i], p[g].astype(BF16)) for g in heads]

    def body(j, carry):
        m, l, acc = carry
        na = 2 * j
        nb_ok = na + 1 < i
        nbk = jnp.where(nb_ok, na + 1, na)
        sa = [_dot_nt(keys(na, g), q[g]) for g in heads]
        sb = [_dot_nt(keys(nbk, g), q[g]) for g in heads]
        m_new, l_new, pa, pb = [], [], [], []
        for g in heads:
            pick_a = sel_ref[g, pl.ds(na, 1), :] > 0.0
            pick_b = (sel_ref[g, pl.ds(nbk, 1), :] > 0.0) & nb_ok
            xa = jnp.where(pick_a, sa[g] * scale, NEG_INF)
            xb = jnp.where(pick_b, sb[g] * scale, NEG_INF)
            mg = jnp.maximum(m[g], jnp.maximum(jnp.max(xa, axis=0, keepdims=True),
                                               jnp.max(xb, axis=0, keepdims=True)))
            ea = jnp.exp2(xa - mg)
            eb = jnp.exp2(xb - mg)
            alpha = jnp.exp2(m[g] - mg)
            m_new.append(mg)
            l_new.append(alpha * l[g] + jnp.sum(ea, axis=0, keepdims=True) + jnp.sum(eb, axis=0, keepdims=True))
            pa.append((alpha, ea.astype(BF16)))
            pb.append(eb.astype(BF16))
        acc_new = [pa[g][0] * acc[g] + _dot(vt_ref[g, na], pa[g][1]) + _dot(vt_ref[g, nbk], pb[g])
                   for g in heads]
        return tuple(m_new), tuple(l_new), tuple(acc_new)

    _, l, acc = lax.fori_loop(0, (i + 1) // 2, body, (tuple(m0), tuple(l0), tuple(acc0)))
    for g in heads:
        o_ref[:, g * hd:(g + 1) * hd] = (acc[g] / l[g]).T.astype(o_ref.dtype)


def _staircase_pairs():
    k = PEER_TOPK
    return [(a, b) for a in range(k) for b in range(k) if (a + 1) * (b + 1) <= k]


def _peer_route_kernel(q_ref, keys_ref, r2_ref, lb_ref, e1_ref, e2_ref, s_scr, rank_scr):
    tm = q_ref.shape[0]
    nk, k = PEER_NKEYS, PEER_TOPK
    rowi = lax.broadcasted_iota(jnp.int32, (nk, tm), 0)
    sub8 = lax.broadcasted_iota(jnp.int32, (PEER_HEADS, tm), 0)

    vals = [[jnp.zeros((PEER_HEADS, tm), F32) for _ in range(k)] for _ in range(2)]
    for hc in range(2 * PEER_HEADS):
        h, c = divmod(hc, 2)
        q_hi, q_mid, q_lo = _split3(q_ref[:, hc * PEER_HALF:(hc + 1) * PEER_HALF])
        kh = keys_ref[hc]
        s = _dot_nt(kh, q_hi) + _dot_nt(kh, q_mid) + _dot_nt(kh, q_lo)
        s_scr[hc] = s
        rank = jnp.full((nk, tm), float(k), F32)
        for j in range(k):
            m = jnp.max(s, axis=0, keepdims=True)
            idx = jnp.min(jnp.where(s == m, rowi, nk), axis=0, keepdims=True)
            pick = rowi == idx
            rank = jnp.where(pick, float(j), rank)
            s = jnp.where(pick, NEG_INF, s)
            vals[c][j] = jnp.where(sub8 == h, jnp.broadcast_to(m, (PEER_HEADS, tm)), vals[c][j])
        rank_scr[hc] = rank

    pairs = _staircase_pairs()
    csum = {p: vals[0][p[0]] + vals[1][p[1]] for p in pairs}
    cnt = {}
    for y in pairs:
        cnt[y] = jnp.full((PEER_HEADS, tm), float((y[0] + 1) * (y[1] + 1) - 1), F32)
    for xi, x in enumerate(pairs):
        for y in pairs[xi + 1:]:
            if x[0] <= y[0] and x[1] <= y[1]:
                continue
            x_first = jnp.where(csum[x] >= csum[y], 1.0, 0.0)
            cnt[y] = cnt[y] + x_first
            cnt[x] = cnt[x] + (1.0 - x_first)
    ex = [[jnp.exp(vals[c][j] - vals[c][0]) for j in range(k)] for c in range(2)]
    length = [jnp.zeros((PEER_HEADS, tm), F32) for _ in range(k)]
    z = jnp.zeros((PEER_HEADS, tm), F32)
    for p in pairs:
        chosen = jnp.where(cnt[p] < float(k), 1.0, 0.0)
        length[p[0]] = length[p[0]] + chosen
        z = z + chosen * (ex[0][p[0]] * ex[1][p[1]])
    inv_z = 1.0 / z

    for h in range(PEER_HEADS):
        r1 = rank_scr[2 * h]
        r2 = rank_scr[2 * h + 1]
        lb = jnp.zeros((nk, tm), F32)
        for a in range(k):
            lb = jnp.where(r1 == float(a), length[a][h:h + 1, :], lb)
        lb_ref[h] = lb
        e1_ref[h] = jnp.where(r1 < float(k), jnp.exp(s_scr[2 * h] - vals[0][0][h:h + 1, :]), 0.0)
        e2 = jnp.exp(s_scr[2 * h + 1] - vals[1][0][h:h + 1, :]) * inv_z[h:h + 1, :]
        e2_ref[h] = jnp.where(r2 < float(k), e2, 0.0).astype(e2_ref.dtype)
        r2_ref[h] = r2.astype(r2_ref.dtype)


def peer_route(qry, sub_keys, tm=256):
    t = qry.shape[0]
    tm = min(tm, t)
    nh, nk = PEER_HEADS, PEER_NKEYS
    keys = sub_keys.reshape(2 * nh, nk, PEER_HALF).astype(BF16)
    out = pl.BlockSpec((nh, nk, tm), lambda i: (0, 0, i))
    return pl.pallas_call(
        _peer_route_kernel,
        grid=(t // tm,),
        in_specs=[pl.BlockSpec((tm, 2 * nh * PEER_HALF), lambda i: (i, 0)),
                  pl.BlockSpec((2 * nh, nk, PEER_HALF), lambda i: (0, 0, 0))],
        out_specs=[out, out, out, out],
        out_shape=[jax.ShapeDtypeStruct((nh, nk, t), BF16), jax.ShapeDtypeStruct((nh, nk, t), F32),
                   jax.ShapeDtypeStruct((nh, nk, t), F32), jax.ShapeDtypeStruct((nh, nk, t), BF16)],
        scratch_shapes=[pltpu.VMEM((2 * nh, nk, tm), F32), pltpu.VMEM((2 * nh, nk, tm), F32)],
        compiler_params=_params(("parallel",)),
        name="peer_route",
    )(qry, keys)


PEER_I1_PER_TILE = 4


def _peer_expert_kernel(x_ref, u_ref, v_ref, r2_ref, lb_ref, e1_ref, e2_ref, o_ref):
    j = pl.program_id(1)
    tm = x_ref.shape[0]
    nk = PEER_NKEYS

    @pl.when(j == 0)
    def _():
        o_ref[...] = jnp.zeros_like(o_ref)

    hid = _dot_nt(u_ref[...], x_ref[...])
    act = 0.5 * hid * (1.0 + lax.erf(hid * (2.0 ** -0.5)))
    base = (j % (8 // PEER_I1_PER_TILE)) * PEER_I1_PER_TILE
    parts = []
    for a in range(PEER_I1_PER_TILE):
        g = jnp.zeros((nk, tm), BF16)
        for h in range(PEER_HEADS):
            lrow = lb_ref[h, pl.ds(base + a, 1), :]
            erow = e1_ref[h, pl.ds(base + a, 1), :]
            lbc = jnp.broadcast_to(lrow, (nk, tm)).astype(BF16)
            e1c = jnp.broadcast_to(erow, (nk, tm)).astype(BF16)
            g = g + jnp.where(r2_ref[h] < lbc, e1c * e2_ref[h], jnp.zeros((nk, tm), BF16))
        parts.append(g)
    gate = jnp.concatenate(parts, axis=0)
    p = (act * gate.astype(F32)).astype(BF16)
    o_ref[...] += _dot_tn(p, v_ref[...])


def peer_experts(hn, u_tab, v_tab, r2, lb, e1, e2, tm=512):
    t, d = hn.shape
    tm = min(tm, t)
    te = PEER_I1_PER_TILE * PEER_NKEYS
    nh, nk = PEER_HEADS, PEER_NKEYS
    full = pl.BlockSpec((nh, nk, tm), lambda i, j: (0, 0, i))
    rows = pl.BlockSpec((nh, 8, tm), lambda i, j: (0, j // (8 // PEER_I1_PER_TILE), i))
    return pl.pallas_call(
        _peer_expert_kernel,
        grid=(t // tm, PEER_EXPERTS // te),
        in_specs=[pl.BlockSpec((tm, d), lambda i, j: (i, 0)),
                  pl.BlockSpec((te, d), lambda i, j: (j, 0)),
                  pl.BlockSpec((te, d), lambda i, j: (j, 0)),
                  full, rows, rows, full],
        out_specs=pl.BlockSpec((tm, d), lambda i, j: (i, 0)),
        out_shape=jax.ShapeDtypeStruct((t, d), F32),
        compiler_params=_params(("parallel", "arbitrary")),
        name="peer_experts",
    )(hn, u_tab, v_tab, r2, lb, e1, e2)


def peer_ffn(hn, w_q, sub_keys, u_tab, v_tab):
    qry = matmul(hn, w_q, F32, 1024, 1024, "peer_query")
    r2, lb, e1, e2 = peer_route(qry, sub_keys)
    return peer_experts(hn, u_tab, v_tab, r2, lb, e1, e2)


def moba_attention(z_moba):
    t = z_moba.shape[0]
    nb = t // MOBA_BLOCK
    g = MOBA_HEADS_PER_STEP
    groups = MOBA_HEADS // g
    gw = g * MOBA_HEAD_DIM
    kmean = moba_block_means(z_moba).reshape(nb, MOBA_WIDTH)
    v_t = moba_transpose_v(z_moba)
    return pl.pallas_call(
        _moba_kernel,
        grid=(groups, nb),
        in_specs=[pl.BlockSpec((MOBA_BLOCK, gw), lambda hg, i: (i, hg)),
                  pl.BlockSpec((t, gw), lambda hg, i: (0, groups + hg)),
                  pl.BlockSpec((g, nb, MOBA_HEAD_DIM, MOBA_BLOCK), lambda hg, i: (hg, 0, 0, 0)),
                  pl.BlockSpec((nb, gw), lambda hg, i: (0, hg))],
        out_specs=pl.BlockSpec((MOBA_BLOCK, gw), lambda hg, i: (i, hg)),
        out_shape=jax.ShapeDtypeStruct((t, MOBA_WIDTH), BF16),
        scratch_shapes=[pltpu.VMEM((g, nb, MOBA_BLOCK), F32)],
        compiler_params=_params(("parallel", "arbitrary")),
        name="moba_attention",
    )(z_moba, z_moba, v_t, kmean)


def kernel(x, ln1_g, w_in, rwkv_mu, rwkv_w0, rwkv_w2, rwkv_a0, rwkv_a2, rwkv_g2, rwkv_k_k, rwkv_k_a,
           rwkv_r_k, rwkv_gn_w, rwkv_gn_b, w_proj_rwkv, w_proj_moba, w_out, ln2_g,
           peer_w_q, peer_sub_keys, peer_u, peer_v, ln_f_g):
    b, s, d = x.shape
    depth = w_in.shape[0]
    h = x.reshape(b * s, d)
    rkv = 3 * RWKV_WIDTH
    for l in range(depth):
        w_in_b = w_in[l].astype(BF16)
        xn = rmsnorm(h, ln1_g[l], BF16)
        z_rkv = matmul(xn, w_in_b[:, :rkv], F32, 1024, 1024, "in_proj_rwkv")
        z_l = matmul(xn, w_in_b[:, rkv:RWKV_COLS], F32, 1024, LORA_COLS, "in_proj_lora")
        z_moba = matmul(xn, w_in_b[:, RWKV_COLS:RWKV_COLS + MOBA_COLS], BF16, 1024, 1024, "in_proj_moba")
        z_gate = matmul(xn, w_in_b[:, RWKV_COLS + MOBA_COLS:], BF16, 1024, 1024, "in_proj_gate", sigmoid=True)
        y_a = rwkv7(z_rkv, z_l, rwkv_mu[l], rwkv_w0[l], rwkv_w2[l], rwkv_a0[l], rwkv_a2[l], rwkv_g2[l],
                    rwkv_k_k[l], rwkv_k_a[l], rwkv_r_k[l], rwkv_gn_w[l], rwkv_gn_b[l])
        y_b = moba_attention(z_moba)
        merged = gated_merge(y_a, y_b, w_proj_rwkv[l].astype(BF16), w_proj_moba[l].astype(BF16), z_gate)
        h = matmul_residual(merged, w_out[l].astype(BF16), h)
        hn = rmsnorm(h, ln2_g[l], BF16)
        y_p = peer_ffn(hn, peer_w_q[l].astype(BF16), peer_sub_keys[l],
                       peer_u[l].astype(BF16), peer_v[l].astype(BF16))
        if l + 1 < depth:
            h = h + y_p
    return add_rmsnorm(h, y_p, ln_f_g).reshape(b, s, d)
```

```python
import functools

import jax
import jax.numpy as jnp
from jax import lax
from jax.experimental import pallas as pl
from jax.experimental.pallas import tpu as pltpu

F32 = jnp.float32
BF16 = jnp.bfloat16

D_MODEL = 4096
RMS_EPS = 1e-6

RWKV_HEADS = 32
RWKV_HEAD_DIM = 64
RWKV_WIDTH = RWKV_HEADS * RWKV_HEAD_DIM
DECAY_LORA = 64
ICLR_LORA = 64
GATE_LORA = 256
LORA_COLS = DECAY_LORA + ICLR_LORA + GATE_LORA
GN_EPS = 64e-5
RWKV_COLS = 3 * RWKV_WIDTH + LORA_COLS
CHUNK = 64
PAIR = 2 * RWKV_HEAD_DIM
N_PAIRS = RWKV_HEADS // 2

MOBA_HEADS = 16
MOBA_HEAD_DIM = 128
MOBA_WIDTH = MOBA_HEADS * MOBA_HEAD_DIM
MOBA_BLOCK = 256
MOBA_TOPK = 3
MOBA_COLS = 3 * MOBA_WIDTH

PEER_HEADS = 8
PEER_NKEYS = 128
PEER_EXPERTS = PEER_NKEYS * PEER_NKEYS
PEER_QDIM = 256
PEER_HALF = PEER_QDIM // 2
PEER_TOPK = 16

VMEM_LIMIT = 56 * 1024 * 1024

NEG_INF = float("-inf")


def _params(sem):
    return pltpu.CompilerParams(dimension_semantics=sem, vmem_limit_bytes=VMEM_LIMIT)


def _dot(a, b):
    return jnp.dot(a, b, preferred_element_type=F32)


def _dot_nt(a, b):
    return lax.dot_general(a, b, (((1,), (1,)), ((), ())), preferred_element_type=F32)


def _dot_tn(a, b):
    return lax.dot_general(a, b, (((0,), (0,)), ((), ())), preferred_element_type=F32)


def _split(x, pieces):
    out = []
    for _ in range(pieces - 1):
        hi = x.astype(BF16)
        out.append(hi)
        x = x - hi.astype(F32)
    out.append(x.astype(BF16))
    return out


def _split3(x):
    return _split(x, 3)


def _dot_exact_rhs(x, e, pieces=3):
    return sum(_dot(p, e) for p in _split(x, pieces))


def _dot_exact_lhs(e, x, pieces=3):
    return sum(_dot(e, p) for p in _split(x, pieces))


def _rmsnorm_kernel(x_ref, g_ref, o_ref):
    x = x_ref[...]
    y = x * lax.rsqrt(jnp.mean(x * x, axis=-1, keepdims=True) + RMS_EPS)
    o_ref[...] = (y * g_ref[...]).astype(o_ref.dtype)


def rmsnorm(x, g, out_dtype, bm=256):
    t, d = x.shape
    return pl.pallas_call(
        _rmsnorm_kernel,
        grid=(t // bm,),
        in_specs=[pl.BlockSpec((bm, d), lambda i: (i, 0)), pl.BlockSpec((1, d), lambda i: (0, 0))],
        out_specs=pl.BlockSpec((bm, d), lambda i: (i, 0)),
        out_shape=jax.ShapeDtypeStruct((t, d), out_dtype),
        compiler_params=_params(("parallel",)),
        name="rmsnorm",
    )(x, g.reshape(1, d))


def _add_rmsnorm_kernel(a_ref, b_ref, g_ref, o_ref):
    x = a_ref[...] + b_ref[...]
    y = x * lax.rsqrt(jnp.mean(x * x, axis=-1, keepdims=True) + RMS_EPS)
    o_ref[...] = (y * g_ref[...]).astype(o_ref.dtype)


def add_rmsnorm(a, b, g, bm=256):
    t, d = a.shape
    return pl.pallas_call(
        _add_rmsnorm_kernel,
        grid=(t // bm,),
        in_specs=[pl.BlockSpec((bm, d), lambda i: (i, 0)), pl.BlockSpec((bm, d), lambda i: (i, 0)),
                  pl.BlockSpec((1, d), lambda i: (0, 0))],
        out_specs=pl.BlockSpec((bm, d), lambda i: (i, 0)),
        out_shape=jax.ShapeDtypeStruct((t, d), F32),
        compiler_params=_params(("parallel",)),
        name="add_rmsnorm",
    )(a, b, g.reshape(1, d))


def _mm_kernel(a_ref, b_ref, o_ref):
    o_ref[...] = _dot(a_ref[...], b_ref[...]).astype(o_ref.dtype)


def _mm_sigmoid_kernel(a_ref, b_ref, o_ref):
    o_ref[...] = jax.nn.sigmoid(_dot(a_ref[...], b_ref[...])).astype(o_ref.dtype)


def matmul(a, b, out_dtype, bm, bn, name, sigmoid=False):
    m, k = a.shape
    n = b.shape[1]
    bm, bn = min(bm, m), min(bn, n)
    return pl.pallas_call(
        _mm_sigmoid_kernel if sigmoid else _mm_kernel,
        grid=(n // bn, m // bm),
        in_specs=[pl.BlockSpec((bm, k), lambda j, i: (i, 0)), pl.BlockSpec((k, bn), lambda j, i: (0, j))],
        out_specs=pl.BlockSpec((bm, bn), lambda j, i: (i, j)),
        out_shape=jax.ShapeDtypeStruct((m, n), out_dtype),
        compiler_params=_params(("parallel", "parallel")),
        name=name,
    )(a, b)


def _merge_kernel(ya_ref, yb_ref, wa_ref, wb_ref, ga_ref, gb_ref, o_ref):
    pa = _dot(ya_ref[...], wa_ref[...])
    pb = _dot(yb_ref[...], wb_ref[...])
    o_ref[...] = (ga_ref[...].astype(F32) * pa + gb_ref[...].astype(F32) * pb).astype(o_ref.dtype)


def gated_merge(y_a, y_b, w_a, w_b, z_gate, bm=1024, bn=1024):
    m, k = y_a.shape
    n = w_a.shape[1]
    bm = min(bm, m)
    nb = n // bn
    return pl.pallas_call(
        _merge_kernel,
        grid=(nb, m // bm),
        in_specs=[pl.BlockSpec((bm, k), lambda j, i: (i, 0)), pl.BlockSpec((bm, k), lambda j, i: (i, 0)),
                  pl.BlockSpec((k, bn), lambda j, i: (0, j)), pl.BlockSpec((k, bn), lambda j, i: (0, j)),
                  pl.BlockSpec((bm, bn), lambda j, i: (i, j)), pl.BlockSpec((bm, bn), lambda j, i: (i, j + nb))],
        out_specs=pl.BlockSpec((bm, bn), lambda j, i: (i, j)),
        out_shape=jax.ShapeDtypeStruct((m, n), BF16),
        compiler_params=_params(("parallel", "parallel")),
        name="gated_merge",
    )(y_a, y_b, w_a, w_b, z_gate, z_gate)


def _mm_residual_kernel(a_ref, b_ref, r_ref, o_ref):
    o_ref[...] = r_ref[...] + _dot(a_ref[...], b_ref[...])


def matmul_residual(a, b, res, bm=1024, bn=1024):
    m, k = a.shape
    n = b.shape[1]
    bm = min(bm, m)
    return pl.pallas_call(
        _mm_residual_kernel,
        grid=(n // bn, m // bm),
        in_specs=[pl.BlockSpec((bm, k), lambda j, i: (i, 0)), pl.BlockSpec((k, bn), lambda j, i: (0, j)),
                  pl.BlockSpec((bm, bn), lambda j, i: (i, j))],
        out_specs=pl.BlockSpec((bm, bn), lambda j, i: (i, j)),
        out_shape=jax.ShapeDtypeStruct((m, n), F32),
        compiler_params=_params(("parallel", "parallel")),
        name="out_proj_residual",
    )(a, b, res)


def _pair_masks(shape):
    lane = lax.broadcasted_iota(jnp.int32, shape, len(shape) - 1)
    return lane < RWKV_HEAD_DIM


def _block_ones():
    r = lax.broadcasted_iota(jnp.int32, (PAIR, PAIR), 0) // RWKV_HEAD_DIM
    c = lax.broadcasted_iota(jnp.int32, (PAIR, PAIR), 1) // RWKV_HEAD_DIM
    return jnp.where(r == c, 1.0, 0.0).astype(BF16)


def _rwkv_local_kernel(zr_ref, zk_ref, zv_ref, zl_ref, pr_ref, pk_ref, pv_ref, pl_ref,
                       mur_ref, muk_ref, muv_ref, mul_ref,
                       w0_ref, w2_ref, a0_ref, a2_ref, g2_ref, kk_ref, ka_ref, rk_ref,
                       ar_ref, vt_ref, y0_ref, arb_ref, bh_ref, kvt_ref, gam_ref, g_ref, bonus_ref):
    first = pl.program_id(1) == 0
    tm = zr_ref.shape[0]
    nc = tm // CHUNK

    def shifted(z_ref, p_ref, mu_ref):
        z = z_ref[...]
        prev_last = jnp.where(first, 0.0, p_ref[7:8, :])
        row = lax.broadcasted_iota(jnp.int32, z.shape, 0)
        zp = jnp.where(row == 0, prev_last, pltpu.roll(z, 1, axis=0))
        return z + (zp - z) * mu_ref[...]

    r = shifted(zr_ref, pr_ref, mur_ref)
    k = shifted(zk_ref, pk_ref, muk_ref)
    v = shifted(zv_ref, pv_ref, muv_ref)
    zl = shifted(zl_ref, pl_ref, mul_ref)
    wd = zl[:, :DECAY_LORA]
    ad = zl[:, DECAY_LORA:DECAY_LORA + ICLR_LORA]
    gd = zl[:, DECAY_LORA + ICLR_LORA:]

    wl = w0_ref[...] + _dot(jnp.tanh(wd).astype(BF16), w2_ref[...])
    logw = -(2.718281828459045 ** -0.5) * jax.nn.sigmoid(wl)
    iclr = jax.nn.sigmoid(a0_ref[...] + _dot(ad.astype(BF16), a2_ref[...]))
    g_ref[...] = _dot(jax.nn.sigmoid(gd).astype(BF16), g2_ref[...])

    ones_blk = _block_ones()
    kk = k * kk_ref[...]
    kk = kk * lax.rsqrt(_dot_exact_rhs(kk * kk, ones_blk, 2) + 1e-12)
    k = k * (1.0 + (iclr - 1.0) * ka_ref[...])
    beta = kk * iclr
    bonus_ref[...] = _dot_exact_rhs(r * k * rk_ref[...], ones_blk, 2) * v

    ti = lax.broadcasted_iota(jnp.int32, (CHUNK, CHUNK), 0)
    si = lax.broadcasted_iota(jnp.int32, (CHUNK, CHUNK), 1)
    strict = ti > si
    incl = ti >= si
    eye = jnp.where(ti == si, 1.0, 0.0)
    head0 = _pair_masks((CHUNK, PAIR))
    head0_x = _pair_masks((2 * CHUNK, PAIR))
    vr = lax.broadcasted_iota(jnp.int32, (PAIR, PAIR), 0) // RWKV_HEAD_DIM
    vc = lax.broadcasted_iota(jnp.int32, (PAIR, PAIR), 1) // RWKV_HEAD_DIM
    same_head = vr == vc

    tt = lax.broadcasted_iota(jnp.int32, (tm, tm), 0)
    ts = lax.broadcasted_iota(jnp.int32, (tm, tm), 1)
    same_chunk = (tt // CHUNK) == (ts // CHUNK)
    cs = _dot_exact_lhs(jnp.where(same_chunk & (tt >= ts), 1.0, 0.0).astype(BF16), logw, 2)
    ctot = jnp.concatenate([jnp.broadcast_to(cs[(c + 1) * CHUNK - 1:(c + 1) * CHUNK], (CHUNK, PAIR))
                            for c in range(nc)], axis=0)
    e_neg = jnp.exp(-cs)
    e_hat = jnp.exp(ctot - cs)
    abar = -kk * jnp.exp(cs - logw)
    rbar = r * jnp.exp(cs)
    bbar = (beta * e_neg).astype(BF16)
    kbar = (k * e_neg).astype(BF16)
    vb = v.astype(BF16)
    abar_b = abar.astype(BF16)
    bh_ref[...] = (beta * e_hat).astype(BF16)
    khat = (k * e_hat).astype(BF16)
    gam = jnp.exp(ctot)

    chunks = [slice(c * CHUNK, (c + 1) * CHUNK) for c in range(nc)]
    probs = [(c, h) for c in range(nc) for h in range(2)]
    xh = {}
    for c, rows in enumerate(chunks):
        x = jnp.concatenate([abar[rows], rbar[rows]], axis=0)
        xh[c, 0] = jnp.where(head0_x, x, 0.0).astype(BF16)
        xh[c, 1] = jnp.where(head0_x, 0.0, x).astype(BF16)
    mb = {q: _dot_nt(xh[q], bbar[chunks[q[0]]]) for q in probs}
    mk = {q: _dot_nt(xh[q], kbar[chunks[q[0]]]) for q in probs}
    npow = {q: jnp.where(strict, mb[q][:CHUNK], 0.0) for q in probs}
    a_ak = {q: jnp.where(strict, mk[q][:CHUNK], 0.0).astype(BF16) for q in probs}
    a_rb = {q: jnp.where(incl, mb[q][CHUNK:], 0.0) for q in probs}
    a_rk = {q: jnp.where(incl, mk[q][CHUNK:], 0.0).astype(BF16) for q in probs}
    tinv = {q: eye + npow[q] for q in probs}
    for _ in range(5):
        nb = {q: npow[q].astype(BF16) for q in probs}
        npow = {q: _dot(nb[q], nb[q]) for q in probs}
        tinv = {q: tinv[q] + _dot(npow[q].astype(BF16), tinv[q].astype(BF16)) for q in probs}
    tb = {q: tinv[q].astype(BF16) for q in probs}
    at = {q: _dot(tb[q], abar_b[chunks[q[0]]]) for q in probs}
    wv = {q: _dot(a_ak[q], vb[chunks[q[0]]]).astype(BF16) for q in probs}
    y0 = {q: _dot(a_rk[q], vb[chunks[q[0]]]) for q in probs}
    vt = {q: _dot(tb[q], wv[q]) for q in probs}
    kvt = [_dot_tn(vb[rows], khat[rows]) for rows in chunks]

    for c, rows in enumerate(chunks):
        atil = jnp.where(head0, at[c, 0], at[c, 1])
        ar_ref[c] = jnp.concatenate([atil, rbar[rows]], axis=0).astype(BF16)
        vt_ref[rows, :] = jnp.where(head0, vt[c, 0], vt[c, 1])
        y0_ref[rows, :] = jnp.where(head0, y0[c, 0], y0[c, 1])
        arb_ref[rows, :] = jnp.concatenate([a_rb[c, 0], a_rb[c, 1]], axis=1).astype(BF16)
        kvt_ref[c] = jnp.where(same_head, kvt[c], 0.0)
        gam_ref[c] = gam[c * CHUNK:c * CHUNK + 1, :]


def rwkv_local(z_rkv, z_l, mu, w0, w2, a0, a2, g2, k_k, k_a, r_k, tm=256):
    t = z_rkv.shape[0]
    nc = tm // CHUNK
    n_chunks = t // CHUNK
    w = RWKV_WIDTH
    mu_rkv = mu[:3 * w].reshape(1, 3 * w)
    mu_l = mu[3 * w:].reshape(1, LORA_COLS)
    rb = tm // 8

    def zspec(off):
        return pl.BlockSpec((tm, PAIR), lambda p, i: (i, p + off))

    def pspec(off):
        return pl.BlockSpec((8, PAIR), lambda p, i: (jnp.maximum(i * rb - 1, 0), p + off))

    def vspec(off=0):
        return pl.BlockSpec((1, PAIR), lambda p, i: (0, p + off))

    row = lambda a: a.reshape(1, w)
    tok = pl.BlockSpec((tm, PAIR), lambda p, i: (i, p))
    chk = pl.BlockSpec((nc, PAIR, PAIR), lambda p, i: (i, 0, p))
    f32 = lambda s: jax.ShapeDtypeStruct(s, F32)
    b16 = lambda s: jax.ShapeDtypeStruct(s, BF16)
    return pl.pallas_call(
        _rwkv_local_kernel,
        grid=(N_PAIRS, t // tm),
        in_specs=[zspec(0), zspec(N_PAIRS), zspec(2 * N_PAIRS),
                  pl.BlockSpec((tm, LORA_COLS), lambda p, i: (i, 0)),
                  pspec(0), pspec(N_PAIRS), pspec(2 * N_PAIRS),
                  pl.BlockSpec((8, LORA_COLS), lambda p, i: (jnp.maximum(i * rb - 1, 0), 0)),
                  vspec(0), vspec(N_PAIRS), vspec(2 * N_PAIRS),
                  pl.BlockSpec((1, LORA_COLS), lambda p, i: (0, 0)),
                  vspec(), pl.BlockSpec((DECAY_LORA, PAIR), lambda p, i: (0, p)),
                  vspec(), pl.BlockSpec((ICLR_LORA, PAIR), lambda p, i: (0, p)),
                  pl.BlockSpec((GATE_LORA, PAIR), lambda p, i: (0, p)),
                  vspec(), vspec(), vspec()],
        out_specs=[chk, tok, tok, tok, tok, chk,
                   pl.BlockSpec((nc, 1, PAIR), lambda p, i: (i, 0, p)), tok, tok],
        out_shape=[b16((n_chunks, PAIR, w)), f32((t, w)), f32((t, w)), b16((t, w)), b16((t, w)),
                   f32((n_chunks, PAIR, w)), f32((n_chunks, 1, w)), f32((t, w)), f32((t, w))],
        compiler_params=_params(("parallel", "parallel")),
        name="rwkv_local",
    )(z_rkv, z_rkv, z_rkv, z_l, z_rkv, z_rkv, z_rkv, z_l,
      mu_rkv, mu_rkv, mu_rkv, mu_l,
      row(w0), w2.astype(BF16), row(a0), a2.astype(BF16), g2.astype(BF16), row(k_k), row(k_a), row(r_k))


def _rwkv_scan_kernel(ar_ref, vt_ref, y0_ref, arb_ref, bh_ref, kvt_ref, gam_ref, g_ref, bonus_ref,
                      gnw_ref, gnb_ref, y_ref, s_ref):
    @pl.when(pl.program_id(0) == 0)
    def _():
        s_ref[...] = jnp.zeros_like(s_ref)

    head0 = _pair_masks((CHUNK, PAIR))
    vr = lax.broadcasted_iota(jnp.int32, (PAIR, PAIR), 0) // RWKV_HEAD_DIM
    vc = lax.broadcasted_iota(jnp.int32, (PAIR, PAIR), 1) // RWKV_HEAD_DIM
    same_head = vr == vc
    ones_blk = _block_ones()
    inv_n = 1.0 / RWKV_HEAD_DIM

    pairs = range(N_PAIRS)
    lanes = [slice(p * PAIR, (p + 1) * PAIR) for p in pairs]
    s = [s_ref[p] for p in pairs]
    xh = [_dot_nt(ar_ref[0, :, lanes[p]], s[p].astype(BF16)) for p in pairs]
    ub = [(xh[p][:CHUNK] + vt_ref[:, lanes[p]]).astype(BF16) for p in pairs]
    yu0 = [_dot(arb_ref[:, p * PAIR:p * PAIR + CHUNK], ub[p]) for p in pairs]
    yu1 = [_dot(arb_ref[:, p * PAIR + CHUNK:(p + 1) * PAIR], ub[p]) for p in pairs]
    su = [_dot_tn(ub[p], bh_ref[:, lanes[p]]) for p in pairs]
    for p in pairs:
        s_ref[p] = s[p] * gam_ref[0, :, lanes[p]] + jnp.where(same_head, su[p], 0.0) + kvt_ref[0, :, lanes[p]]
    y = [xh[p][CHUNK:] + jnp.where(head0, yu0[p], yu1[p]) + y0_ref[:, lanes[p]] for p in pairs]
    mom = [_dot_exact_rhs(jnp.concatenate([y[p], y[p] * y[p]], axis=0), ones_blk) * inv_n for p in pairs]
    for p in pairs:
        mean = mom[p][:CHUNK]
        var = mom[p][CHUNK:] - mean * mean
        yn = (y[p] - mean) * lax.rsqrt(var + GN_EPS) * gnw_ref[:, lanes[p]] + gnb_ref[:, lanes[p]]
        y_ref[:, lanes[p]] = ((yn + bonus_ref[:, lanes[p]]) * g_ref[:, lanes[p]]).astype(y_ref.dtype)


def rwkv_scan(ar, vt, y0, arb, bh, kvt, gam, g, bonus, gn_w, gn_b):
    t, w = vt.shape
    n_chunks = t // CHUNK
    tok = pl.BlockSpec((CHUNK, w), lambda c: (c, 0))
    chk = pl.BlockSpec((1, PAIR, w), lambda c: (c, 0, 0))
    vec = pl.BlockSpec((1, w), lambda c: (0, 0))
    return pl.pallas_call(
        _rwkv_scan_kernel,
        grid=(n_chunks,),
        in_specs=[chk, tok, tok, tok, tok, chk, pl.BlockSpec((1, 1, w), lambda c: (c, 0, 0)), tok, tok, vec, vec],
        out_specs=tok,
        out_shape=jax.ShapeDtypeStruct((t, w), BF16),
        scratch_shapes=[pltpu.VMEM((N_PAIRS, PAIR, PAIR), F32)],
        compiler_params=_params(("arbitrary",)),
        name="rwkv_scan",
    )(ar, vt, y0, arb, bh, kvt, gam, g, bonus, gn_w.reshape(1, w), gn_b.reshape(1, w))


def rwkv7(z_rkv, z_l, mu, w0, w2, a0, a2, g2, k_k, k_a, r_k, gn_w, gn_b):
    ar, vt, y0, arb, bh, kvt, gam, g, bonus = rwkv_local(
        z_rkv, z_l, mu, w0, w2, a0, a2, g2, k_k, k_a, r_k.reshape(-1))
    return rwkv_scan(ar, vt, y0, arb, bh, kvt, gam, g, bonus, gn_w, gn_b)


def _block_mean_kernel(k_ref, o_ref):
    o_ref[0] = jnp.mean(k_ref[...].astype(F32), axis=0, keepdims=True)


def moba_block_means(z_moba):
    t = z_moba.shape[0]
    nb = t // MOBA_BLOCK
    return pl.pallas_call(
        _block_mean_kernel,
        grid=(nb,),
        in_specs=[pl.BlockSpec((MOBA_BLOCK, MOBA_WIDTH), lambda i: (i, 1))],
        out_specs=pl.BlockSpec((1, 1, MOBA_WIDTH), lambda i: (i, 0, 0)),
        out_shape=jax.ShapeDtypeStruct((nb, 1, MOBA_WIDTH), F32),
        compiler_params=_params(("parallel",)),
        name="moba_block_means",
    )(z_moba)


MOBA_VT_ROWS = MOBA_HEAD_DIM + 16


def _transpose_v_kernel(v_ref, o_ref):
    ones = jnp.ones((MOBA_VT_ROWS - MOBA_HEAD_DIM, MOBA_BLOCK), o_ref.dtype)
    for h in range(MOBA_HEADS):
        vh = v_ref[:, h * MOBA_HEAD_DIM:(h + 1) * MOBA_HEAD_DIM].astype(F32)
        o_ref[h, 0] = jnp.concatenate([vh.T.astype(o_ref.dtype), ones], axis=0)


def moba_transpose_v(z_moba):
    t = z_moba.shape[0]
    nb = t // MOBA_BLOCK
    return pl.pallas_call(
        _transpose_v_kernel,
        grid=(nb,),
        in_specs=[pl.BlockSpec((MOBA_BLOCK, MOBA_WIDTH), lambda i: (i, 2))],
        out_specs=pl.BlockSpec((MOBA_HEADS, 1, MOBA_VT_ROWS, MOBA_BLOCK), lambda i: (0, i, 0, 0)),
        out_shape=jax.ShapeDtypeStruct((MOBA_HEADS, nb, MOBA_VT_ROWS, MOBA_BLOCK), z_moba.dtype),
        compiler_params=_params(("parallel",)),
        name="moba_transpose_v",
    )(z_moba)


MOBA_HEADS_PER_STEP = 4


def _moba_kernel(q_ref, k_ref, vt_ref, km_ref, o_ref, sel_ref):
    i = pl.program_id(1)
    bs, hd = MOBA_BLOCK, MOBA_HEAD_DIM
    nb = km_ref.shape[0]
    heads = range(MOBA_HEADS_PER_STEP)
    scale = hd ** -0.5 * 1.4426950408889634
    q = [q_ref[:, g * hd:(g + 1) * hd] for g in heads]

    blk = lax.broadcasted_iota(jnp.int32, (nb, bs), 0)
    for g in heads:
        km_hi, km_mid, km_lo = _split3(km_ref[:, g * hd:(g + 1) * hd])
        gate = _dot_nt(km_hi, q[g]) + _dot_nt(km_mid, q[g]) + _dot_nt(km_lo, q[g])
        gate = jnp.where(blk < i, gate, NEG_INF)
        sel = jnp.zeros((nb, bs), F32)
        for _ in range(MOBA_TOPK):
            m = jnp.max(gate, axis=0, keepdims=True)
            cand = jnp.where((gate == m) & (m > NEG_INF), blk, nb)
            pick = blk == jnp.min(cand, axis=0, keepdims=True)
            sel = jnp.where(pick, 1.0, sel)
            gate = jnp.where(pick, NEG_INF, gate)
        sel_ref[g] = sel

    def keys(n, g):
        return k_ref[pl.ds(pl.multiple_of(n * bs, bs), bs), g * hd:(g + 1) * hd]

    ki = lax.broadcasted_iota(jnp.int32, (bs, bs), 0)
    qi = lax.broadcasted_iota(jnp.int32, (bs, bs), 1)
    s = [jnp.where(ki <= qi, _dot_nt(keys(i, g), q[g]) * scale, NEG_INF) for g in heads]
    m0 = [jnp.max(s[g], axis=0, keepdims=True) for g in heads]
    p = [jnp.exp2((s[g] - m0[g]).astype(BF16)) for g in heads]
    acc0 = [_dot(vt_ref[g, i], p[g]) for g in heads]

    def body(j, carry):
        m, acc = carry
        na = 2 * j
        nb_ok = na + 1 < i
        nbk = jnp.where(nb_ok, na + 1, na)
        sa = [_dot_nt(keys(na, g), q[g]) for g in heads]
        sb = [_dot_nt(keys(nbk, g), q[g]) for g in heads]
        m_new, alpha, pa, pb = [], [], [], []
        for g in heads:
            pick_a = sel_ref[g, pl.ds(na, 1), :] > 0.0
            pick_b = (sel_ref[g, pl.ds(nbk, 1), :] > 0.0) & nb_ok
            xa = jnp.where(pick_a, sa[g] * scale, NEG_INF)
            xb = jnp.where(pick_b, sb[g] * scale, NEG_INF)
            mg = jnp.maximum(m[g], jnp.maximum(jnp.max(xa, axis=0, keepdims=True),
                                               jnp.max(xb, axis=0, keepdims=True)))
            m_new.append(mg)
            alpha.append(jnp.exp2(m[g] - mg))
            pa.append(jnp.exp2((xa - mg).astype(BF16)))
            pb.append(jnp.exp2((xb - mg).astype(BF16)))
        acc_new = [alpha[g] * acc[g] + _dot(vt_ref[g, na], pa[g]) + _dot(vt_ref[g, nbk], pb[g])
                   for g in heads]
        return tuple(m_new), tuple(acc_new)

    _, acc = lax.fori_loop(0, (i + 1) // 2, body, (tuple(m0), tuple(acc0)))
    for g in heads:
        out = acc[g][:hd] / acc[g][hd:hd + 1]
        o_ref[:, g * hd:(g + 1) * hd] = out.T.astype(o_ref.dtype)


def _staircase_pairs():
    k = PEER_TOPK
    return [(a, b) for a in range(k) for b in range(k) if (a + 1) * (b + 1) <= k]


def _peer_route_kernel(q_ref, keys_ref, r2_ref, lb_ref, e1_ref, e2_ref, s_scr, rank_scr, vals_scr):
    tm = q_ref.shape[0]
    nk, k = PEER_NKEYS, PEER_TOPK
    rowf = lax.broadcasted_iota(jnp.int32, (nk, tm), 0).astype(F32)

    for hc in range(2 * PEER_HEADS):
        q_hi, q_mid, q_lo = _split3(q_ref[:, hc * PEER_HALF:(hc + 1) * PEER_HALF])
        kh = keys_ref[hc]
        s_scr[hc] = _dot_nt(kh, q_hi) + _dot_nt(kh, q_mid) + _dot_nt(kh, q_lo)

    def extract_all(break_ties):
        miscount = jnp.zeros((1, tm), F32)
        for hc in range(2 * PEER_HEADS):
            h, c = divmod(hc, 2)
            s = s_scr[hc]
            rank = jnp.full((nk, tm), float(k), F32)
            for j in range(k):
                m = jnp.max(s, axis=0, keepdims=True)
                pick = s == m
                if break_ties:
                    pick = rowf == jnp.min(jnp.where(pick, rowf, float(nk)), axis=0, keepdims=True)
                rank = jnp.where(pick, float(j), rank)
                s = jnp.where(pick, NEG_INF, s)
                vals_scr[c, j, h:h + 1, :] = m
            rank_scr[hc] = rank
            n_sel = jnp.sum(jnp.where(rank < float(k), 1.0, 0.0), axis=0, keepdims=True)
            miscount = miscount + jnp.abs(n_sel - float(k))
        return miscount

    miscount = extract_all(False)

    @pl.when(jnp.max(miscount) > 0.0)
    def _():
        extract_all(True)

    vals = [[vals_scr[c, j] for j in range(k)] for c in range(2)]

    pairs = _staircase_pairs()
    csum = {p: vals[0][p[0]] + vals[1][p[1]] for p in pairs}
    cnt = {}
    for y in pairs:
        cnt[y] = jnp.full((PEER_HEADS, tm), float((y[0] + 1) * (y[1] + 1) - 1), F32)
    for xi, x in enumerate(pairs):
        for y in pairs[xi + 1:]:
            if x[0] <= y[0] and x[1] <= y[1]:
                continue
            x_first = jnp.where(csum[x] >= csum[y], 1.0, 0.0)
            cnt[y] = cnt[y] + x_first
            cnt[x] = cnt[x] + (1.0 - x_first)
    ex = [[jnp.exp(vals[c][j] - vals[c][0]) for j in range(k)] for c in range(2)]
    length = [jnp.zeros((PEER_HEADS, tm), F32) for _ in range(k)]
    z = jnp.zeros((PEER_HEADS, tm), F32)
    for p in pairs:
        chosen = jnp.where(cnt[p] < float(k), 1.0, 0.0)
        length[p[0]] = length[p[0]] + chosen
        z = z + chosen * (ex[0][p[0]] * ex[1][p[1]])
    inv_z = 1.0 / z

    for h in range(PEER_HEADS):
        r1 = rank_scr[2 * h]
        r2 = rank_scr[2 * h + 1]
        lb = jnp.zeros((nk, tm), F32)
        for a in range(k):
            lb = jnp.where(r1 == float(a), length[a][h:h + 1, :], lb)
        lb_ref[h] = lb
        e1_ref[h] = jnp.where(r1 < float(k), jnp.exp(s_scr[2 * h] - vals[0][0][h:h + 1, :]), 0.0)
        e2 = jnp.exp(s_scr[2 * h + 1] - vals[1][0][h:h + 1, :]) * inv_z[h:h + 1, :]
        e2_ref[h] = jnp.where(r2 < float(k), e2, 0.0).astype(e2_ref.dtype)
        r2_ref[h] = r2.astype(r2_ref.dtype)


def peer_route(qry, sub_keys, tm=256):
    t = qry.shape[0]
    tm = min(tm, t)
    nh, nk = PEER_HEADS, PEER_NKEYS
    keys = sub_keys.reshape(2 * nh, nk, PEER_HALF).astype(BF16)
    out = pl.BlockSpec((nh, nk, tm), lambda i: (0, 0, i))
    return pl.pallas_call(
        _peer_route_kernel,
        grid=(t // tm,),
        in_specs=[pl.BlockSpec((tm, 2 * nh * PEER_HALF), lambda i: (i, 0)),
                  pl.BlockSpec((2 * nh, nk, PEER_HALF), lambda i: (0, 0, 0))],
        out_specs=[out, out, out, out],
        out_shape=[jax.ShapeDtypeStruct((nh, nk, t), BF16), jax.ShapeDtypeStruct((nh, nk, t), F32),
                   jax.ShapeDtypeStruct((nh, nk, t), F32), jax.ShapeDtypeStruct((nh, nk, t), BF16)],
        scratch_shapes=[pltpu.VMEM((2 * nh, nk, tm), F32), pltpu.VMEM((2 * nh, nk, tm), F32),
                        pltpu.VMEM((2, PEER_TOPK, nh, tm), F32)],
        compiler_params=_params(("parallel",)),
        name="peer_route",
    )(qry, keys)


PEER_I1_PER_TILE = 4


def _peer_expert_kernel(x_ref, u_ref, v_ref, r2_ref, lb_ref, e1_ref, e2_ref, o_ref):
    j = pl.program_id(1)
    tm = x_ref.shape[0]
    nk = PEER_NKEYS

    @pl.when(j == 0)
    def _():
        o_ref[...] = jnp.zeros_like(o_ref)

    hid = _dot_nt(u_ref[...], x_ref[...])
    act = 0.5 * hid * (1.0 + lax.erf(hid * (2.0 ** -0.5)))
    base = (j % (8 // PEER_I1_PER_TILE)) * PEER_I1_PER_TILE
    parts = []
    for a in range(PEER_I1_PER_TILE):
        g = jnp.zeros((nk, tm), BF16)
        for h in range(PEER_HEADS):
            lrow = lb_ref[h, pl.ds(base + a, 1), :]
            erow = e1_ref[h, pl.ds(base + a, 1), :]
            lbc = jnp.broadcast_to(lrow, (nk, tm)).astype(BF16)
            e1c = jnp.broadcast_to(erow, (nk, tm)).astype(BF16)
            g = g + jnp.where(r2_ref[h] < lbc, e1c * e2_ref[h], jnp.zeros((nk, tm), BF16))
        parts.append(g)
    gate = jnp.concatenate(parts, axis=0)
    o_ref[...] += _dot_tn(act.astype(BF16) * gate, v_ref[...])


def peer_experts(hn, u_tab, v_tab, r2, lb, e1, e2, tm=512):
    t, d = hn.shape
    tm = min(tm, t)
    te = PEER_I1_PER_TILE * PEER_NKEYS
    nh, nk = PEER_HEADS, PEER_NKEYS
    full = pl.BlockSpec((nh, nk, tm), lambda i, j: (0, 0, i))
    rows = pl.BlockSpec((nh, 8, tm), lambda i, j: (0, j // (8 // PEER_I1_PER_TILE), i))
    return pl.pallas_call(
        _peer_expert_kernel,
        grid=(t // tm, PEER_EXPERTS // te),
        in_specs=[pl.BlockSpec((tm, d), lambda i, j: (i, 0)),
                  pl.BlockSpec((te, d), lambda i, j: (j, 0)),
                  pl.BlockSpec((te, d), lambda i, j: (j, 0)),
                  full, rows, rows, full],
        out_specs=pl.BlockSpec((tm, d), lambda i, j: (i, 0)),
        out_shape=jax.ShapeDtypeStruct((t, d), F32),
        compiler_params=_params(("parallel", "arbitrary")),
        name="peer_experts",
    )(hn, u_tab, v_tab, r2, lb, e1, e2)


def peer_ffn(hn, w_q, sub_keys, u_tab, v_tab):
    qry = matmul(hn, w_q, F32, 1024, 1024, "peer_query")
    r2, lb, e1, e2 = peer_route(qry, sub_keys)
    return peer_experts(hn, u_tab, v_tab, r2, lb, e1, e2)


def moba_attention(z_moba):
    t = z_moba.shape[0]
    nb = t // MOBA_BLOCK
    g = MOBA_HEADS_PER_STEP
    groups = MOBA_HEADS // g
    gw = g * MOBA_HEAD_DIM
    kmean = moba_block_means(z_moba).reshape(nb, MOBA_WIDTH)
    v_t = moba_transpose_v(z_moba)
    return pl.pallas_call(
        _moba_kernel,
        grid=(groups, nb),
        in_specs=[pl.BlockSpec((MOBA_BLOCK, gw), lambda hg, i: (i, hg)),
                  pl.BlockSpec((t, gw), lambda hg, i: (0, groups + hg)),
                  pl.BlockSpec((g, nb, MOBA_VT_ROWS, MOBA_BLOCK), lambda hg, i: (hg, 0, 0, 0)),
                  pl.BlockSpec((nb, gw), lambda hg, i: (0, hg))],
        out_specs=pl.BlockSpec((MOBA_BLOCK, gw), lambda hg, i: (i, hg)),
        out_shape=jax.ShapeDtypeStruct((t, MOBA_WIDTH), BF16),
        scratch_shapes=[pltpu.VMEM((g, nb, MOBA_BLOCK), F32)],
        compiler_params=_params(("parallel", "arbitrary")),
        name="moba_attention",
    )(z_moba, z_moba, v_t, kmean)


def kernel(x, ln1_g, w_in, rwkv_mu, rwkv_w0, rwkv_w2, rwkv_a0, rwkv_a2, rwkv_g2, rwkv_k_k, rwkv_k_a,
           rwkv_r_k, rwkv_gn_w, rwkv_gn_b, w_proj_rwkv, w_proj_moba, w_out, ln2_g,
           peer_w_q, peer_sub_keys, peer_u, peer_v, ln_f_g):
    b, s, d = x.shape
    depth = w_in.shape[0]
    h = x.reshape(b * s, d)
    rkv = 3 * RWKV_WIDTH
    for l in range(depth):
        w_cols = lambda a, b: w_in[l][:, a:b].astype(BF16)
        xn = rmsnorm(h, ln1_g[l], BF16)
        z_rkv = matmul(xn, w_cols(0, rkv), F32, 1024, 1024, "in_proj_rwkv")
        z_l = matmul(xn, w_cols(rkv, RWKV_COLS), F32, 1024, LORA_COLS, "in_proj_lora")
        z_moba = matmul(xn, w_cols(RWKV_COLS, RWKV_COLS + MOBA_COLS), BF16, 1024, 1024, "in_proj_moba")
        z_gate = matmul(xn, w_cols(RWKV_COLS + MOBA_COLS, RWKV_COLS + MOBA_COLS + 2 * d), BF16, 1024, 1024,
                        "in_proj_gate", sigmoid=True)
        y_a = rwkv7(z_rkv, z_l, rwkv_mu[l], rwkv_w0[l], rwkv_w2[l], rwkv_a0[l], rwkv_a2[l], rwkv_g2[l],
                    rwkv_k_k[l], rwkv_k_a[l], rwkv_r_k[l], rwkv_gn_w[l], rwkv_gn_b[l])
        y_b = moba_attention(z_moba)
        merged = gated_merge(y_a, y_b, w_proj_rwkv[l].astype(BF16), w_proj_moba[l].astype(BF16), z_gate)
        h = matmul_residual(merged, w_out[l].astype(BF16), h)
        hn = rmsnorm(h, ln2_g[l], BF16)
        y_p = peer_ffn(hn, peer_w_q[l].astype(BF16), peer_sub_keys[l],
                       peer_u[l].astype(BF16), peer_v[l].astype(BF16))
        if l + 1 < depth:
            h = h + y_p
    return add_rmsnorm(h, y_p, ln_f_g).reshape(b, s, d)
```

```python
import functools

import jax
import jax.numpy as jnp
from jax import lax
from jax.experimental import pallas as pl
from jax.experimental.pallas import tpu as pltpu

F32 = jnp.float32
BF16 = jnp.bfloat16

D_MODEL = 4096
RMS_EPS = 1e-6

RWKV_HEADS = 32
RWKV_HEAD_DIM = 64
RWKV_WIDTH = RWKV_HEADS * RWKV_HEAD_DIM
DECAY_LORA = 64
ICLR_LORA = 64
GATE_LORA = 256
LORA_COLS = DECAY_LORA + ICLR_LORA + GATE_LORA
GN_EPS = 64e-5
RWKV_COLS = 3 * RWKV_WIDTH + LORA_COLS
CHUNK = 64
PAIR = 2 * RWKV_HEAD_DIM
N_PAIRS = RWKV_HEADS // 2

MOBA_HEADS = 16
MOBA_HEAD_DIM = 128
MOBA_WIDTH = MOBA_HEADS * MOBA_HEAD_DIM
MOBA_BLOCK = 256
MOBA_TOPK = 3
MOBA_COLS = 3 * MOBA_WIDTH

PEER_HEADS = 8
PEER_NKEYS = 128
PEER_EXPERTS = PEER_NKEYS * PEER_NKEYS
PEER_QDIM = 256
PEER_HALF = PEER_QDIM // 2
PEER_TOPK = 16

VMEM_LIMIT = 56 * 1024 * 1024

NEG_INF = float("-inf")


def _params(sem):
    return pltpu.CompilerParams(dimension_semantics=sem, vmem_limit_bytes=VMEM_LIMIT)


def _dot(a, b):
    return jnp.dot(a, b, preferred_element_type=F32)


def _dot_nt(a, b):
    return lax.dot_general(a, b, (((1,), (1,)), ((), ())), preferred_element_type=F32)


def _dot_tn(a, b):
    return lax.dot_general(a, b, (((0,), (0,)), ((), ())), preferred_element_type=F32)


def _split(x, pieces):
    out = []
    for _ in range(pieces - 1):
        hi = x.astype(BF16)
        out.append(hi)
        x = x - hi.astype(F32)
    out.append(x.astype(BF16))
    return out


def _split3(x):
    return _split(x, 3)


def _dot_exact_rhs(x, e, pieces=3):
    return sum(_dot(p, e) for p in _split(x, pieces))


def _dot_exact_lhs(e, x, pieces=3):
    return sum(_dot(e, p) for p in _split(x, pieces))


def _rmsnorm_kernel(x_ref, g_ref, o_ref):
    x = x_ref[...]
    y = x * lax.rsqrt(jnp.mean(x * x, axis=-1, keepdims=True) + RMS_EPS)
    o_ref[...] = (y * g_ref[...]).astype(o_ref.dtype)


def rmsnorm(x, g, out_dtype, bm=256):
    t, d = x.shape
    return pl.pallas_call(
        _rmsnorm_kernel,
        grid=(t // bm,),
        in_specs=[pl.BlockSpec((bm, d), lambda i: (i, 0)), pl.BlockSpec((1, d), lambda i: (0, 0))],
        out_specs=pl.BlockSpec((bm, d), lambda i: (i, 0)),
        out_shape=jax.ShapeDtypeStruct((t, d), out_dtype),
        compiler_params=_params(("parallel",)),
        name="rmsnorm",
    )(x, g.reshape(1, d))


def _add_rmsnorm_kernel(a_ref, b_ref, g_ref, o_ref):
    x = a_ref[...] + b_ref[...]
    y = x * lax.rsqrt(jnp.mean(x * x, axis=-1, keepdims=True) + RMS_EPS)
    o_ref[...] = (y * g_ref[...]).astype(o_ref.dtype)


def add_rmsnorm(a, b, g, bm=256):
    t, d = a.shape
    return pl.pallas_call(
        _add_rmsnorm_kernel,
        grid=(t // bm,),
        in_specs=[pl.BlockSpec((bm, d), lambda i: (i, 0)), pl.BlockSpec((bm, d), lambda i: (i, 0)),
                  pl.BlockSpec((1, d), lambda i: (0, 0))],
        out_specs=pl.BlockSpec((bm, d), lambda i: (i, 0)),
        out_shape=jax.ShapeDtypeStruct((t, d), F32),
        compiler_params=_params(("parallel",)),
        name="add_rmsnorm",
    )(a, b, g.reshape(1, d))


def _mm_kernel(a_ref, b_ref, o_ref):
    o_ref[...] = _dot(a_ref[...], b_ref[...]).astype(o_ref.dtype)


def _mm_sigmoid_kernel(a_ref, b_ref, o_ref):
    o_ref[...] = jax.nn.sigmoid(_dot(a_ref[...], b_ref[...])).astype(o_ref.dtype)


def matmul(a, b, out_dtype, bm, bn, name, sigmoid=False):
    m, k = a.shape
    n = b.shape[1]
    bm, bn = min(bm, m), min(bn, n)
    return pl.pallas_call(
        _mm_sigmoid_kernel if sigmoid else _mm_kernel,
        grid=(n // bn, m // bm),
        in_specs=[pl.BlockSpec((bm, k), lambda j, i: (i, 0)), pl.BlockSpec((k, bn), lambda j, i: (0, j))],
        out_specs=pl.BlockSpec((bm, bn), lambda j, i: (i, j)),
        out_shape=jax.ShapeDtypeStruct((m, n), out_dtype),
        compiler_params=_params(("parallel", "parallel")),
        name=name,
    )(a, b)


def _merge_kernel(ya_ref, yb_ref, wa_ref, wb_ref, ga_ref, gb_ref, o_ref):
    pa = _dot(ya_ref[...], wa_ref[...])
    pb = _dot(yb_ref[...], wb_ref[...])
    o_ref[...] = (ga_ref[...].astype(F32) * pa + gb_ref[...].astype(F32) * pb).astype(o_ref.dtype)


def gated_merge(y_a, y_b, w_a, w_b, z_gate, bm=1024, bn=1024):
    m, k = y_a.shape
    n = w_a.shape[1]
    bm = min(bm, m)
    nb = n // bn
    return pl.pallas_call(
        _merge_kernel,
        grid=(nb, m // bm),
        in_specs=[pl.BlockSpec((bm, k), lambda j, i: (i, 0)), pl.BlockSpec((bm, k), lambda j, i: (i, 0)),
                  pl.BlockSpec((k, bn), lambda j, i: (0, j)), pl.BlockSpec((k, bn), lambda j, i: (0, j)),
                  pl.BlockSpec((bm, bn), lambda j, i: (i, j)), pl.BlockSpec((bm, bn), lambda j, i: (i, j + nb))],
        out_specs=pl.BlockSpec((bm, bn), lambda j, i: (i, j)),
        out_shape=jax.ShapeDtypeStruct((m, n), BF16),
        compiler_params=_params(("parallel", "parallel")),
        name="gated_merge",
    )(y_a, y_b, w_a, w_b, z_gate, z_gate)


def _mm_residual_kernel(a_ref, b_ref, r_ref, o_ref):
    o_ref[...] = r_ref[...] + _dot(a_ref[...], b_ref[...])


def matmul_residual(a, b, res, bm=1024, bn=1024):
    m, k = a.shape
    n = b.shape[1]
    bm = min(bm, m)
    return pl.pallas_call(
        _mm_residual_kernel,
        grid=(n // bn, m // bm),
        in_specs=[pl.BlockSpec((bm, k), lambda j, i: (i, 0)), pl.BlockSpec((k, bn), lambda j, i: (0, j)),
                  pl.BlockSpec((bm, bn), lambda j, i: (i, j))],
        out_specs=pl.BlockSpec((bm, bn), lambda j, i: (i, j)),
        out_shape=jax.ShapeDtypeStruct((m, n), F32),
        compiler_params=_params(("parallel", "parallel")),
        name="out_proj_residual",
    )(a, b, res)


def _pair_masks(shape):
    lane = lax.broadcasted_iota(jnp.int32, shape, len(shape) - 1)
    return lane < RWKV_HEAD_DIM


def _block_ones():
    r = lax.broadcasted_iota(jnp.int32, (PAIR, PAIR), 0) // RWKV_HEAD_DIM
    c = lax.broadcasted_iota(jnp.int32, (PAIR, PAIR), 1) // RWKV_HEAD_DIM
    return jnp.where(r == c, 1.0, 0.0).astype(BF16)


def _rwkv_local_kernel(zr_ref, zk_ref, zv_ref, zl_ref, pr_ref, pk_ref, pv_ref, pl_ref,
                       mur_ref, muk_ref, muv_ref, mul_ref,
                       w0_ref, w2_ref, a0_ref, a2_ref, g2_ref, kk_ref, ka_ref, rk_ref,
                       ar_ref, vt_ref, y0_ref, arb_ref, bh_ref, kvt_ref, gam_ref, g_ref, bonus_ref):
    first = pl.program_id(1) == 0
    tm = zr_ref.shape[0]
    nc = tm // CHUNK

    def shifted(z_ref, p_ref, mu_ref):
        z = z_ref[...]
        prev_last = jnp.where(first, 0.0, p_ref[7:8, :])
        row = lax.broadcasted_iota(jnp.int32, z.shape, 0)
        zp = jnp.where(row == 0, prev_last, pltpu.roll(z, 1, axis=0))
        return z + (zp - z) * mu_ref[...]

    r = shifted(zr_ref, pr_ref, mur_ref)
    k = shifted(zk_ref, pk_ref, muk_ref)
    v = shifted(zv_ref, pv_ref, muv_ref)
    zl = shifted(zl_ref, pl_ref, mul_ref)
    wd = zl[:, :DECAY_LORA]
    ad = zl[:, DECAY_LORA:DECAY_LORA + ICLR_LORA]
    gd = zl[:, DECAY_LORA + ICLR_LORA:]

    wl = w0_ref[...] + _dot(jnp.tanh(wd).astype(BF16), w2_ref[...])
    logw = -(2.718281828459045 ** -0.5) * jax.nn.sigmoid(wl)
    iclr = jax.nn.sigmoid(a0_ref[...] + _dot(ad.astype(BF16), a2_ref[...]))
    g_ref[...] = _dot(jax.nn.sigmoid(gd).astype(BF16), g2_ref[...])

    ones_blk = _block_ones()
    kk = k * kk_ref[...]
    kk = kk * lax.rsqrt(_dot_exact_rhs(kk * kk, ones_blk, 2) + 1e-12)
    k = k * (1.0 + (iclr - 1.0) * ka_ref[...])
    beta = kk * iclr
    bonus_ref[...] = _dot_exact_rhs(r * k * rk_ref[...], ones_blk, 2) * v

    ti = lax.broadcasted_iota(jnp.int32, (CHUNK, CHUNK), 0)
    si = lax.broadcasted_iota(jnp.int32, (CHUNK, CHUNK), 1)
    strict = ti > si
    incl = ti >= si
    eye = jnp.where(ti == si, 1.0, 0.0)
    head0 = _pair_masks((CHUNK, PAIR))
    head0_x = _pair_masks((2 * CHUNK, PAIR))
    vr = lax.broadcasted_iota(jnp.int32, (PAIR, PAIR), 0) // RWKV_HEAD_DIM
    vc = lax.broadcasted_iota(jnp.int32, (PAIR, PAIR), 1) // RWKV_HEAD_DIM
    same_head = vr == vc

    tt = lax.broadcasted_iota(jnp.int32, (tm, tm), 0)
    ts = lax.broadcasted_iota(jnp.int32, (tm, tm), 1)
    same_chunk = (tt // CHUNK) == (ts // CHUNK)
    cs = _dot_exact_lhs(jnp.where(same_chunk & (tt >= ts), 1.0, 0.0).astype(BF16), logw, 2)
    ctot = jnp.concatenate([jnp.broadcast_to(cs[(c + 1) * CHUNK - 1:(c + 1) * CHUNK], (CHUNK, PAIR))
                            for c in range(nc)], axis=0)
    e_neg = jnp.exp(-cs)
    e_hat = jnp.exp(ctot - cs)
    abar = -kk * jnp.exp(cs - logw)
    rbar = r * jnp.exp(cs)
    bbar = (beta * e_neg).astype(BF16)
    kbar = (k * e_neg).astype(BF16)
    vb = v.astype(BF16)
    abar_b = abar.astype(BF16)
    bh_ref[...] = (beta * e_hat).astype(BF16)
    khat = (k * e_hat).astype(BF16)
    gam = jnp.exp(ctot)

    chunks = [slice(c * CHUNK, (c + 1) * CHUNK) for c in range(nc)]
    probs = [(c, h) for c in range(nc) for h in range(2)]
    xh = {}
    for c, rows in enumerate(chunks):
        x = jnp.concatenate([abar[rows], rbar[rows]], axis=0)
        xh[c, 0] = jnp.where(head0_x, x, 0.0).astype(BF16)
        xh[c, 1] = jnp.where(head0_x, 0.0, x).astype(BF16)
    mb = {q: _dot_nt(xh[q], bbar[chunks[q[0]]]) for q in probs}
    mk = {q: _dot_nt(xh[q], kbar[chunks[q[0]]]) for q in probs}
    npow = {q: jnp.where(strict, mb[q][:CHUNK], 0.0) for q in probs}
    a_ak = {q: jnp.where(strict, mk[q][:CHUNK], 0.0).astype(BF16) for q in probs}
    a_rb = {q: jnp.where(incl, mb[q][CHUNK:], 0.0) for q in probs}
    a_rk = {q: jnp.where(incl, mk[q][CHUNK:], 0.0).astype(BF16) for q in probs}
    tinv = {q: eye + npow[q] for q in probs}
    for _ in range(5):
        nb = {q: npow[q].astype(BF16) for q in probs}
        npow = {q: _dot(nb[q], nb[q]) for q in probs}
        tinv = {q: tinv[q] + _dot(npow[q].astype(BF16), tinv[q].astype(BF16)) for q in probs}
    tb = {q: tinv[q].astype(BF16) for q in probs}
    at = {q: _dot(tb[q], abar_b[chunks[q[0]]]) for q in probs}
    wv = {q: _dot(a_ak[q], vb[chunks[q[0]]]).astype(BF16) for q in probs}
    y0 = {q: _dot(a_rk[q], vb[chunks[q[0]]]) for q in probs}
    vt = {q: _dot(tb[q], wv[q]) for q in probs}
    kvt = [_dot_tn(vb[rows], khat[rows]) for rows in chunks]

    for c, rows in enumerate(chunks):
        atil = jnp.where(head0, at[c, 0], at[c, 1])
        ar_ref[c] = jnp.concatenate([atil, rbar[rows]], axis=0).astype(BF16)
        vt_ref[rows, :] = jnp.where(head0, vt[c, 0], vt[c, 1])
        y0_ref[rows, :] = jnp.where(head0, y0[c, 0], y0[c, 1])
        arb_ref[rows, :] = jnp.concatenate([a_rb[c, 0], a_rb[c, 1]], axis=1).astype(BF16)
        kvt_ref[c] = jnp.where(same_head, kvt[c], 0.0)
        gam_ref[c] = gam[c * CHUNK:c * CHUNK + 1, :]


def rwkv_local(z_rkv, z_l, mu, w0, w2, a0, a2, g2, k_k, k_a, r_k, tm=512):
    t = z_rkv.shape[0]
    nc = tm // CHUNK
    n_chunks = t // CHUNK
    w = RWKV_WIDTH
    mu_rkv = mu[:3 * w].reshape(1, 3 * w)
    mu_l = mu[3 * w:].reshape(1, LORA_COLS)
    rb = tm // 8

    def zspec(off):
        return pl.BlockSpec((tm, PAIR), lambda p, i: (i, p + off))

    def pspec(off):
        return pl.BlockSpec((8, PAIR), lambda p, i: (jnp.maximum(i * rb - 1, 0), p + off))

    def vspec(off=0):
        return pl.BlockSpec((1, PAIR), lambda p, i: (0, p + off))

    row = lambda a: a.reshape(1, w)
    tok = pl.BlockSpec((tm, PAIR), lambda p, i: (i, p))
    chk = pl.BlockSpec((nc, PAIR, PAIR), lambda p, i: (i, 0, p))
    f32 = lambda s: jax.ShapeDtypeStruct(s, F32)
    b16 = lambda s: jax.ShapeDtypeStruct(s, BF16)
    return pl.pallas_call(
        _rwkv_local_kernel,
        grid=(N_PAIRS, t // tm),
        in_specs=[zspec(0), zspec(N_PAIRS), zspec(2 * N_PAIRS),
                  pl.BlockSpec((tm, LORA_COLS), lambda p, i: (i, 0)),
                  pspec(0), pspec(N_PAIRS), pspec(2 * N_PAIRS),
                  pl.BlockSpec((8, LORA_COLS), lambda p, i: (jnp.maximum(i * rb - 1, 0), 0)),
                  vspec(0), vspec(N_PAIRS), vspec(2 * N_PAIRS),
                  pl.BlockSpec((1, LORA_COLS), lambda p, i: (0, 0)),
                  vspec(), pl.BlockSpec((DECAY_LORA, PAIR), lambda p, i: (0, p)),
                  vspec(), pl.BlockSpec((ICLR_LORA, PAIR), lambda p, i: (0, p)),
                  pl.BlockSpec((GATE_LORA, PAIR), lambda p, i: (0, p)),
                  vspec(), vspec(), vspec()],
        out_specs=[chk, tok, tok, tok, tok, chk,
                   pl.BlockSpec((nc, 1, PAIR), lambda p, i: (i, 0, p)), tok, tok],
        out_shape=[b16((n_chunks, PAIR, w)), f32((t, w)), f32((t, w)), b16((t, w)), b16((t, w)),
                   f32((n_chunks, PAIR, w)), f32((n_chunks, 1, w)), f32((t, w)), f32((t, w))],
        compiler_params=_params(("parallel", "parallel")),
        name="rwkv_local",
    )(z_rkv, z_rkv, z_rkv, z_l, z_rkv, z_rkv, z_rkv, z_l,
      mu_rkv, mu_rkv, mu_rkv, mu_l,
      row(w0), w2.astype(BF16), row(a0), a2.astype(BF16), g2.astype(BF16), row(k_k), row(k_a), row(r_k))


def _rwkv_scan_kernel(ar_ref, vt_ref, y0_ref, arb_ref, bh_ref, kvt_ref, gam_ref, g_ref, bonus_ref,
                      gnw_ref, gnb_ref, y_ref, s_ref):
    @pl.when(pl.program_id(0) == 0)
    def _():
        s_ref[...] = jnp.zeros_like(s_ref)

    head0 = _pair_masks((CHUNK, PAIR))
    vr = lax.broadcasted_iota(jnp.int32, (PAIR, PAIR), 0) // RWKV_HEAD_DIM
    vc = lax.broadcasted_iota(jnp.int32, (PAIR, PAIR), 1) // RWKV_HEAD_DIM
    same_head = vr == vc
    ones_blk = _block_ones()
    inv_n = 1.0 / RWKV_HEAD_DIM

    pairs = range(N_PAIRS)
    lanes = [slice(p * PAIR, (p + 1) * PAIR) for p in pairs]
    s = [s_ref[p] for p in pairs]
    xh = [_dot_nt(ar_ref[0, :, lanes[p]], s[p].astype(BF16)) for p in pairs]
    ub = [(xh[p][:CHUNK] + vt_ref[:, lanes[p]]).astype(BF16) for p in pairs]
    yu0 = [_dot(arb_ref[:, p * PAIR:p * PAIR + CHUNK], ub[p]) for p in pairs]
    yu1 = [_dot(arb_ref[:, p * PAIR + CHUNK:(p + 1) * PAIR], ub[p]) for p in pairs]
    su = [_dot_tn(ub[p], bh_ref[:, lanes[p]]) for p in pairs]
    for p in pairs:
        s_ref[p] = s[p] * gam_ref[0, :, lanes[p]] + jnp.where(same_head, su[p], 0.0) + kvt_ref[0, :, lanes[p]]
    y = [xh[p][CHUNK:] + jnp.where(head0, yu0[p], yu1[p]) + y0_ref[:, lanes[p]] for p in pairs]
    mom = [_dot_exact_rhs(jnp.concatenate([y[p], y[p] * y[p]], axis=0), ones_blk) * inv_n for p in pairs]
    for p in pairs:
        mean = mom[p][:CHUNK]
        var = mom[p][CHUNK:] - mean * mean
        yn = (y[p] - mean) * lax.rsqrt(var + GN_EPS) * gnw_ref[:, lanes[p]] + gnb_ref[:, lanes[p]]
        y_ref[:, lanes[p]] = ((yn + bonus_ref[:, lanes[p]]) * g_ref[:, lanes[p]]).astype(y_ref.dtype)


def rwkv_scan(ar, vt, y0, arb, bh, kvt, gam, g, bonus, gn_w, gn_b):
    t, w = vt.shape
    n_chunks = t // CHUNK
    tok = pl.BlockSpec((CHUNK, w), lambda c: (c, 0))
    chk = pl.BlockSpec((1, PAIR, w), lambda c: (c, 0, 0))
    vec = pl.BlockSpec((1, w), lambda c: (0, 0))
    return pl.pallas_call(
        _rwkv_scan_kernel,
        grid=(n_chunks,),
        in_specs=[chk, tok, tok, tok, tok, chk, pl.BlockSpec((1, 1, w), lambda c: (c, 0, 0)), tok, tok, vec, vec],
        out_specs=tok,
        out_shape=jax.ShapeDtypeStruct((t, w), BF16),
        scratch_shapes=[pltpu.VMEM((N_PAIRS, PAIR, PAIR), F32)],
        compiler_params=_params(("arbitrary",)),
        name="rwkv_scan",
    )(ar, vt, y0, arb, bh, kvt, gam, g, bonus, gn_w.reshape(1, w), gn_b.reshape(1, w))


def rwkv7(z_rkv, z_l, mu, w0, w2, a0, a2, g2, k_k, k_a, r_k, gn_w, gn_b):
    ar, vt, y0, arb, bh, kvt, gam, g, bonus = rwkv_local(
        z_rkv, z_l, mu, w0, w2, a0, a2, g2, k_k, k_a, r_k.reshape(-1))
    return rwkv_scan(ar, vt, y0, arb, bh, kvt, gam, g, bonus, gn_w, gn_b)


def _block_mean_kernel(k_ref, o_ref):
    o_ref[0] = jnp.mean(k_ref[...].astype(F32), axis=0, keepdims=True)


def moba_block_means(z_moba):
    t = z_moba.shape[0]
    nb = t // MOBA_BLOCK
    return pl.pallas_call(
        _block_mean_kernel,
        grid=(nb,),
        in_specs=[pl.BlockSpec((MOBA_BLOCK, MOBA_WIDTH), lambda i: (i, 1))],
        out_specs=pl.BlockSpec((1, 1, MOBA_WIDTH), lambda i: (i, 0, 0)),
        out_shape=jax.ShapeDtypeStruct((nb, 1, MOBA_WIDTH), F32),
        compiler_params=_params(("parallel",)),
        name="moba_block_means",
    )(z_moba)


MOBA_VT_ROWS = MOBA_HEAD_DIM + 16


def _transpose_v_kernel(v_ref, o_ref):
    ones = jnp.ones((MOBA_VT_ROWS - MOBA_HEAD_DIM, MOBA_BLOCK), o_ref.dtype)
    for h in range(MOBA_HEADS):
        vh = v_ref[:, h * MOBA_HEAD_DIM:(h + 1) * MOBA_HEAD_DIM].astype(F32)
        o_ref[h, 0] = jnp.concatenate([vh.T.astype(o_ref.dtype), ones], axis=0)


def moba_transpose_v(z_moba):
    t = z_moba.shape[0]
    nb = t // MOBA_BLOCK
    return pl.pallas_call(
        _transpose_v_kernel,
        grid=(nb,),
        in_specs=[pl.BlockSpec((MOBA_BLOCK, MOBA_WIDTH), lambda i: (i, 2))],
        out_specs=pl.BlockSpec((MOBA_HEADS, 1, MOBA_VT_ROWS, MOBA_BLOCK), lambda i: (0, i, 0, 0)),
        out_shape=jax.ShapeDtypeStruct((MOBA_HEADS, nb, MOBA_VT_ROWS, MOBA_BLOCK), z_moba.dtype),
        compiler_params=_params(("parallel",)),
        name="moba_transpose_v",
    )(z_moba)


MOBA_HEADS_PER_STEP = 4


def _moba_kernel(q_ref, k_ref, vt_ref, km_ref, o_ref, sel_ref):
    i = pl.program_id(1)
    bs, hd = MOBA_BLOCK, MOBA_HEAD_DIM
    nb = km_ref.shape[0]
    heads = range(MOBA_HEADS_PER_STEP)
    scale = hd ** -0.5 * 1.4426950408889634
    q = [q_ref[:, g * hd:(g + 1) * hd] for g in heads]

    blk = lax.broadcasted_iota(jnp.int32, (nb, bs), 0)
    for g in heads:
        km_hi, km_mid, km_lo = _split3(km_ref[:, g * hd:(g + 1) * hd])
        gate = _dot_nt(km_hi, q[g]) + _dot_nt(km_mid, q[g]) + _dot_nt(km_lo, q[g])
        gate = jnp.where(blk < i, gate, NEG_INF)
        sel = jnp.zeros((nb, bs), F32)
        for _ in range(MOBA_TOPK):
            m = jnp.max(gate, axis=0, keepdims=True)
            cand = jnp.where((gate == m) & (m > NEG_INF), blk, nb)
            pick = blk == jnp.min(cand, axis=0, keepdims=True)
            sel = jnp.where(pick, 1.0, sel)
            gate = jnp.where(pick, NEG_INF, gate)
        sel_ref[g] = sel

    def keys(n, g):
        return k_ref[pl.ds(pl.multiple_of(n * bs, bs), bs), g * hd:(g + 1) * hd]

    ki = lax.broadcasted_iota(jnp.int32, (bs, bs), 0)
    qi = lax.broadcasted_iota(jnp.int32, (bs, bs), 1)
    s = [jnp.where(ki <= qi, _dot_nt(keys(i, g), q[g]) * scale, NEG_INF) for g in heads]
    m0 = [jnp.max(s[g], axis=0, keepdims=True) for g in heads]
    p = [jnp.exp2((s[g] - m0[g]).astype(BF16)) for g in heads]
    acc0 = [_dot(vt_ref[g, i], p[g]) for g in heads]

    def body(j, carry):
        m, acc = carry
        na = 2 * j
        nb_ok = na + 1 < i
        nbk = jnp.where(nb_ok, na + 1, na)
        sa = [_dot_nt(keys(na, g), q[g]) for g in heads]
        sb = [_dot_nt(keys(nbk, g), q[g]) for g in heads]
        m_new, alpha, pa, pb = [], [], [], []
        for g in heads:
            pick_a = sel_ref[g, pl.ds(na, 1), :] > 0.0
            pick_b = (sel_ref[g, pl.ds(nbk, 1), :] > 0.0) & nb_ok
            xa = jnp.where(pick_a, sa[g] * scale, NEG_INF)
            xb = jnp.where(pick_b, sb[g] * scale, NEG_INF)
            mg = jnp.maximum(m[g], jnp.maximum(jnp.max(xa, axis=0, keepdims=True),
                                               jnp.max(xb, axis=0, keepdims=True)))
            m_new.append(mg)
            alpha.append(jnp.exp2(m[g] - mg))
            pa.append(jnp.exp2((xa - mg).astype(BF16)))
            pb.append(jnp.exp2((xb - mg).astype(BF16)))
        acc_new = [alpha[g] * acc[g] + _dot(vt_ref[g, na], pa[g]) + _dot(vt_ref[g, nbk], pb[g])
                   for g in heads]
        return tuple(m_new), tuple(acc_new)

    _, acc = lax.fori_loop(0, (i + 1) // 2, body, (tuple(m0), tuple(acc0)))
    for g in heads:
        out = acc[g][:hd] / acc[g][hd:hd + 1]
        o_ref[:, g * hd:(g + 1) * hd] = out.T.astype(o_ref.dtype)


def _staircase_pairs():
    k = PEER_TOPK
    return [(a, b) for a in range(k) for b in range(k) if (a + 1) * (b + 1) <= k]


def _peer_route_kernel(q_ref, keys_ref, r2_ref, lb_ref, e1_ref, e2_ref, s_scr, rank_scr, vals_scr):
    tm = q_ref.shape[0]
    nk, k = PEER_NKEYS, PEER_TOPK
    rowf = lax.broadcasted_iota(jnp.int32, (nk, tm), 0).astype(F32)

    for hc in range(2 * PEER_HEADS):
        q_hi, q_mid, q_lo = _split3(q_ref[:, hc * PEER_HALF:(hc + 1) * PEER_HALF])
        kh = keys_ref[hc]
        s_scr[hc] = _dot_nt(kh, q_hi) + _dot_nt(kh, q_mid) + _dot_nt(kh, q_lo)

    def extract_all(break_ties):
        miscount = jnp.zeros((1, tm), F32)
        for hc in range(2 * PEER_HEADS):
            h, c = divmod(hc, 2)
            s = s_scr[hc]
            rank = jnp.full((nk, tm), float(k), F32)
            for j in range(k):
                m = jnp.max(s, axis=0, keepdims=True)
                pick = s == m
                if break_ties:
                    pick = rowf == jnp.min(jnp.where(pick, rowf, float(nk)), axis=0, keepdims=True)
                rank = jnp.where(pick, float(j), rank)
                s = jnp.where(pick, NEG_INF, s)
                vals_scr[c, j, h:h + 1, :] = m
            rank_scr[hc] = rank
            n_sel = jnp.sum(jnp.where(rank < float(k), 1.0, 0.0), axis=0, keepdims=True)
            miscount = miscount + jnp.abs(n_sel - float(k))
        return miscount

    miscount = extract_all(False)

    @pl.when(jnp.max(miscount) > 0.0)
    def _():
        extract_all(True)

    vals = [[vals_scr[c, j] for j in range(k)] for c in range(2)]

    pairs = _staircase_pairs()
    csum = {p: vals[0][p[0]] + vals[1][p[1]] for p in pairs}
    cnt = {}
    for y in pairs:
        cnt[y] = jnp.full((PEER_HEADS, tm), float((y[0] + 1) * (y[1] + 1) - 1), F32)
    for xi, x in enumerate(pairs):
        for y in pairs[xi + 1:]:
            if x[0] <= y[0] and x[1] <= y[1]:
                continue
            x_first = jnp.where(csum[x] >= csum[y], 1.0, 0.0)
            cnt[y] = cnt[y] + x_first
            cnt[x] = cnt[x] + (1.0 - x_first)
    ex = [[jnp.exp(vals[c][j] - vals[c][0]) for j in range(k)] for c in range(2)]
    length = [jnp.zeros((PEER_HEADS, tm), F32) for _ in range(k)]
    z = jnp.zeros((PEER_HEADS, tm), F32)
    for p in pairs:
        chosen = jnp.where(cnt[p] < float(k), 1.0, 0.0)
        length[p[0]] = length[p[0]] + chosen
        z = z + chosen * (ex[0][p[0]] * ex[1][p[1]])
    inv_z = 1.0 / z

    for h in range(PEER_HEADS):
        r1 = rank_scr[2 * h]
        r2 = rank_scr[2 * h + 1]
        lb = jnp.zeros((nk, tm), F32)
        for a in range(k):
            lb = jnp.where(r1 == float(a), length[a][h:h + 1, :], lb)
        lb_ref[h] = lb
        e1_ref[h] = jnp.where(r1 < float(k), jnp.exp(s_scr[2 * h] - vals[0][0][h:h + 1, :]), 0.0)
        e2 = jnp.exp(s_scr[2 * h + 1] - vals[1][0][h:h + 1, :]) * inv_z[h:h + 1, :]
        e2_ref[h] = jnp.where(r2 < float(k), e2, 0.0).astype(e2_ref.dtype)
        r2_ref[h] = r2.astype(r2_ref.dtype)


def peer_route(qry, sub_keys, tm=256):
    t = qry.shape[0]
    tm = min(tm, t)
    nh, nk = PEER_HEADS, PEER_NKEYS
    keys = sub_keys.reshape(2 * nh, nk, PEER_HALF).astype(BF16)
    out = pl.BlockSpec((nh, nk, tm), lambda i: (0, 0, i))
    return pl.pallas_call(
        _peer_route_kernel,
        grid=(t // tm,),
        in_specs=[pl.BlockSpec((tm, 2 * nh * PEER_HALF), lambda i: (i, 0)),
                  pl.BlockSpec((2 * nh, nk, PEER_HALF), lambda i: (0, 0, 0))],
        out_specs=[out, out, out, out],
        out_shape=[jax.ShapeDtypeStruct((nh, nk, t), BF16), jax.ShapeDtypeStruct((nh, nk, t), F32),
                   jax.ShapeDtypeStruct((nh, nk, t), F32), jax.ShapeDtypeStruct((nh, nk, t), BF16)],
        scratch_shapes=[pltpu.VMEM((2 * nh, nk, tm), F32), pltpu.VMEM((2 * nh, nk, tm), F32),
                        pltpu.VMEM((2, PEER_TOPK, nh, tm), F32)],
        compiler_params=_params(("parallel",)),
        name="peer_route",
    )(qry, keys)


PEER_I1_PER_TILE = 4


def _peer_expert_kernel(x_ref, u_ref, v_ref, r2_ref, lb_ref, e1_ref, e2_ref, o_ref):
    j = pl.program_id(1)
    tm = x_ref.shape[0]
    nk = PEER_NKEYS

    @pl.when(j == 0)
    def _():
        o_ref[...] = jnp.zeros_like(o_ref)

    hid = _dot_nt(u_ref[...], x_ref[...])
    act = 0.5 * hid * (1.0 + lax.erf(hid * (2.0 ** -0.5)))
    base = (j % (8 // PEER_I1_PER_TILE)) * PEER_I1_PER_TILE
    parts = []
    for a in range(PEER_I1_PER_TILE):
        g = jnp.zeros((nk, tm), BF16)
        for h in range(PEER_HEADS):
            lrow = lb_ref[h, pl.ds(base + a, 1), :]
            erow = e1_ref[h, pl.ds(base + a, 1), :]
            lbc = jnp.broadcast_to(lrow, (nk, tm)).astype(BF16)
            e1c = jnp.broadcast_to(erow, (nk, tm)).astype(BF16)
            g = g + jnp.where(r2_ref[h] < lbc, e1c * e2_ref[h], jnp.zeros((nk, tm), BF16))
        parts.append(g)
    gate = jnp.concatenate(parts, axis=0)
    o_ref[...] += _dot_tn(act.astype(BF16) * gate, v_ref[...])


def peer_experts(hn, u_tab, v_tab, r2, lb, e1, e2, tm=512):
    t, d = hn.shape
    tm = min(tm, t)
    te = PEER_I1_PER_TILE * PEER_NKEYS
    nh, nk = PEER_HEADS, PEER_NKEYS
    full = pl.BlockSpec((nh, nk, tm), lambda i, j: (0, 0, i))
    rows = pl.BlockSpec((nh, 8, tm), lambda i, j: (0, j // (8 // PEER_I1_PER_TILE), i))
    return pl.pallas_call(
        _peer_expert_kernel,
        grid=(t // tm, PEER_EXPERTS // te),
        in_specs=[pl.BlockSpec((tm, d), lambda i, j: (i, 0)),
                  pl.BlockSpec((te, d), lambda i, j: (j, 0)),
                  pl.BlockSpec((te, d), lambda i, j: (j, 0)),
                  full, rows, rows, full],
        out_specs=pl.BlockSpec((tm, d), lambda i, j: (i, 0)),
        out_shape=jax.ShapeDtypeStruct((t, d), F32),
        compiler_params=_params(("parallel", "arbitrary")),
        name="peer_experts",
    )(hn, u_tab, v_tab, r2, lb, e1, e2)


def peer_ffn(hn, w_q, sub_keys, u_tab, v_tab):
    qry = matmul(hn, w_q, F32, 1024, 1024, "peer_query")
    r2, lb, e1, e2 = peer_route(qry, sub_keys)
    return peer_experts(hn, u_tab, v_tab, r2, lb, e1, e2)


def moba_attention(z_moba):
    t = z_moba.shape[0]
    nb = t // MOBA_BLOCK
    g = MOBA_HEADS_PER_STEP
    groups = MOBA_HEADS // g
    gw = g * MOBA_HEAD_DIM
    kmean = moba_block_means(z_moba).reshape(nb, MOBA_WIDTH)
    v_t = moba_transpose_v(z_moba)
    return pl.pallas_call(
        _moba_kernel,
        grid=(groups, nb),
        in_specs=[pl.BlockSpec((MOBA_BLOCK, gw), lambda hg, i: (i, hg)),
                  pl.BlockSpec((t, gw), lambda hg, i: (0, groups + hg)),
                  pl.BlockSpec((g, nb, MOBA_VT_ROWS, MOBA_BLOCK), lambda hg, i: (hg, 0, 0, 0)),
                  pl.BlockSpec((nb, gw), lambda hg, i: (0, hg))],
        out_specs=pl.BlockSpec((MOBA_BLOCK, gw), lambda hg, i: (i, hg)),
        out_shape=jax.ShapeDtypeStruct((t, MOBA_WIDTH), BF16),
        scratch_shapes=[pltpu.VMEM((g, nb, MOBA_BLOCK), F32)],
        compiler_params=_params(("parallel", "arbitrary")),
        name="moba_attention",
    )(z_moba, z_moba, v_t, kmean)


def kernel(x, ln1_g, w_in, rwkv_mu, rwkv_w0, rwkv_w2, rwkv_a0, rwkv_a2, rwkv_g2, rwkv_k_k, rwkv_k_a,
           rwkv_r_k, rwkv_gn_w, rwkv_gn_b, w_proj_rwkv, w_proj_moba, w_out, ln2_g,
           peer_w_q, peer_sub_keys, peer_u, peer_v, ln_f_g):
    b, s, d = x.shape
    depth = w_in.shape[0]
    h = x.reshape(b * s, d)
    rkv = 3 * RWKV_WIDTH
    for l in range(depth):
        w_cols = lambda a, b: w_in[l][:, a:b].astype(BF16)
        xn = rmsnorm(h, ln1_g[l], BF16)
        z_rkv = matmul(xn, w_cols(0, rkv), F32, 1024, 1024, "in_proj_rwkv")
        z_l = matmul(xn, w_cols(rkv, RWKV_COLS), F32, 1024, LORA_COLS, "in_proj_lora")
        z_moba = matmul(xn, w_cols(RWKV_COLS, RWKV_COLS + MOBA_COLS), BF16, 1024, 1024, "in_proj_moba")
        z_gate = matmul(xn, w_cols(RWKV_COLS + MOBA_COLS, RWKV_COLS + MOBA_COLS + 2 * d), BF16, 1024, 1024,
                        "in_proj_gate", sigmoid=True)
        y_a = rwkv7(z_rkv, z_l, rwkv_mu[l], rwkv_w0[l], rwkv_w2[l], rwkv_a0[l], rwkv_a2[l], rwkv_g2[l],
                    rwkv_k_k[l], rwkv_k_a[l], rwkv_r_k[l], rwkv_gn_w[l], rwkv_gn_b[l])
        y_b = moba_attention(z_moba)
        merged = gated_merge(y_a, y_b, w_proj_rwkv[l].astype(BF16), w_proj_moba[l].astype(BF16), z_gate)
        h = matmul_residual(merged, w_out[l].astype(BF16), h)
        hn = rmsnorm(h, ln2_g[l], BF16)
        y_p = peer_ffn(hn, peer_w_q[l].astype(BF16), peer_sub_keys[l],
                       peer_u[l].astype(BF16), peer_v[l].astype(BF16))
        if l + 1 < depth:
            h = h + y_p
    return add_rmsnorm(h, y_p, ln_f_g).reshape(b, s, d)
```
